```python
import math
import jax, jax.numpy as jnp
from jax import lax
import numpy as np

D_MODEL = 2048
BATCH = 2
SEQ = 4096
DEPTH = 4

N_MIXERS = 2
N_LAYERS_A = (DEPTH + 1) // 2
N_LAYERS_B = DEPTH // 2

MLA_HEADS = 16
Q_LORA = 512
KV_LORA = 512
QK_NOPE = 128
QK_ROPE = 64
V_HEAD = 128
ROPE_THETA = 10000.0
Q_BLOCK = 128

FNET_GROUPS = 4
FNET_GROUP_DIM = D_MODEL // FNET_GROUPS

N_EXPERTS = 32
TOP_K = 4
D_EXPERT = D_MODEL // 2
SWIGLU_LIMIT = 7.0
SWIGLU_ALPHA = 1.702
MOE_BLOCK = 128

DEEPNORM_ALPHA = (2 * DEPTH) ** 0.25
DEEPNORM_BETA = (8 * DEPTH) ** -0.25
LN_EPS = 1e-5
RMS_EPS = 1e-6

kernel_name = "hybrid_mla_fnet_moe_deepnorm_encoder"


def layer_norm(x, g, b):
    xf = x.astype(jnp.float32)
    mu = jnp.mean(xf, axis=-1, keepdims=True)
    var = jnp.mean(jnp.square(xf - mu), axis=-1, keepdims=True)
    return ((xf - mu) * lax.rsqrt(var + LN_EPS) * g.astype(jnp.float32) + b.astype(jnp.float32)).astype(x.dtype)


def rms_norm(x, g):
    xf = x.astype(jnp.float32)
    ms = jnp.mean(jnp.square(xf), axis=-1, keepdims=True)
    return (xf * lax.rsqrt(ms + RMS_EPS) * g.astype(jnp.float32)).astype(x.dtype)


def rope_tables(positions):
    inv_freq = 1.0 / (ROPE_THETA ** (jnp.arange(0, QK_ROPE, 2, dtype=jnp.float32) / QK_ROPE))
    ang = positions.astype(jnp.float32)[..., None] * inv_freq
    return jnp.cos(ang), jnp.sin(ang)


def apply_rope(x, cos, sin):
    xf = x.astype(jnp.float32)
    x1, x2 = xf[..., 0::2], xf[..., 1::2]
    out = jnp.stack([x1 * cos - x2 * sin, x1 * sin + x2 * cos], axis=-1)
    return out.reshape(x.shape).astype(x.dtype)


def mla_mixer(x, cos, sin, w_dq, q_norm_g, w_uq, w_dkv, kv_norm_g, w_ukv, w_o):
    B, S, _ = x.shape
    H = MLA_HEADS
    c_q = rms_norm(x @ w_dq, q_norm_g)
    q = (c_q @ w_uq).reshape(B, S, H, QK_NOPE + QK_ROPE)
    q_nope = q[..., :QK_NOPE]
    q_pe = apply_rope(q[..., QK_NOPE:], cos[:, :, None, :], sin[:, :, None, :])
    kv_a = x @ w_dkv
    c_kv = rms_norm(kv_a[..., :KV_LORA], kv_norm_g)
    k_pe = apply_rope(kv_a[..., KV_LORA:], cos, sin)
    kv = (c_kv @ w_ukv).reshape(B, S, H, QK_NOPE + V_HEAD)
    k_nope, v = kv[..., :QK_NOPE], kv[..., QK_NOPE:]
    scale = (QK_NOPE + QK_ROPE) ** -0.5
    n_blk = S // Q_BLOCK
    qn_blocks = jnp.moveaxis(q_nope.reshape(B, n_blk, Q_BLOCK, H, QK_NOPE), 1, 0)
    qp_blocks = jnp.moveaxis(q_pe.reshape(B, n_blk, Q_BLOCK, H, QK_ROPE), 1, 0)

    def attend(blk):
        qn, qp = blk
        s = (jnp.einsum('bqhd,bkhd->bhqk', qn, k_nope)
             + jnp.einsum('bqhr,bkr->bhqk', qp, k_pe))
        p = jax.nn.softmax(s.astype(jnp.float32) * scale, axis=-1).astype(v.dtype)
        return jnp.einsum('bhqk,bkhd->bqhd', p, v)

    o = lax.map(attend, (qn_blocks, qp_blocks))
    o = jnp.moveaxis(o, 0, 1).reshape(B, S, H * V_HEAD)
    return o @ w_o


def fourier_mixer(x, w_f, b_f):
    B, S, D = x.shape
    xg = x.astype(jnp.float32).reshape(B, S, FNET_GROUPS, FNET_GROUP_DIM)
    mixed = jnp.fft.fft2(xg, axes=(1, 3), norm='ortho').real
    mixed = mixed.astype(x.dtype).reshape(B, S, D)
    return mixed @ w_f + b_f


def clamped_swiglu(h_gate, h_lin):
    g = jnp.minimum(h_gate, SWIGLU_LIMIT)
    l = jnp.clip(h_lin, -SWIGLU_LIMIT, SWIGLU_LIMIT)
    return g * jax.nn.sigmoid(SWIGLU_ALPHA * g) * (l + 1.0)


def moe(x, w_router, b_router, w_gate, b_gate, w_up, b_up, w_down, b_down):
    B, S, D = x.shape
    n_tok = B * S
    n_assign = n_tok * TOP_K
    xt = x.reshape(n_tok, D)
    logits = (xt @ w_router + b_router).astype(jnp.float32)
    top_val, top_idx = lax.top_k(logits, TOP_K)
    gates = jax.nn.softmax(top_val, axis=-1)
    flat_e = top_idx.reshape(-1)
    flat_tok = jnp.arange(n_assign, dtype=jnp.int32) // TOP_K
    flat_gate = gates.reshape(-1)
    order = jnp.argsort(flat_e)
    sorted_e = flat_e[order]
    counts = jnp.bincount(flat_e, length=N_EXPERTS)
    padded = (counts + MOE_BLOCK - 1) // MOE_BLOCK * MOE_BLOCK
    start = jnp.cumsum(counts) - counts
    pend = jnp.cumsum(padded)
    pstart = pend - padded
    rank = jnp.arange(n_assign, dtype=jnp.int32) - start[sorted_e]
    dest = pstart[sorted_e] + rank
    n_rows = n_assign + N_EXPERTS * MOE_BLOCK
    n_blocks = n_rows // MOE_BLOCK
    row_tok = jnp.zeros((n_rows,), jnp.int32).at[dest].set(flat_tok[order])
    row_gate = jnp.zeros((n_rows,), jnp.float32).at[dest].set(flat_gate[order])
    block_e = jnp.minimum(
        jnp.searchsorted(pend, jnp.arange(n_blocks, dtype=jnp.int32) * MOE_BLOCK, side='right'),
        N_EXPERTS - 1).astype(jnp.int32)
    xs = xt[row_tok].reshape(n_blocks, MOE_BLOCK, D)

    def expert_block(args):
        xb, e = args
        h_g = xb @ w_gate[e] + b_gate[e]
        h_u = xb @ w_up[e] + b_up[e]
        return clamped_swiglu(h_g, h_u) @ w_down[e] + b_down[e]

    ys = lax.map(expert_block, (xs, block_e)).reshape(n_rows, D)
    y = jax.ops.segment_sum(ys * row_gate[:, None].astype(ys.dtype), row_tok, num_segments=n_tok)
    return y.reshape(B, S, D)


def setup_inputs(seed: int = 0) -> dict:
    key = jax.random.key(seed)
    ks = jax.random.split(key, 32)
    f32 = jnp.float32
    nrm = lambda k, shape, s: jax.random.normal(k, shape, f32) * s
    D, H, E, F = D_MODEL, MLA_HEADS, N_EXPERTS, D_EXPERT
    nA, nB = N_LAYERS_A, N_LAYERS_B
    x = jax.random.normal(ks[0], (BATCH, SEQ, D), f32)
    positions = (jnp.arange(SEQ, dtype=jnp.int32)[None, :]
                 + jax.random.randint(ks[1], (BATCH, 1), 0, SEQ, dtype=jnp.int32))
    return {
        "x": x,
        "positions": positions,
        "ln_in_g": 1.0 + nrm(ks[2], (D,), 0.02),
        "ln_in_b": nrm(ks[3], (D,), 0.02),
        "mla_w_dq": nrm(ks[4], (nA, D, Q_LORA), D ** -0.5),
        "mla_q_norm": 1.0 + nrm(ks[5], (nA, Q_LORA), 0.02),
        "mla_w_uq": nrm(ks[6], (nA, Q_LORA, H * (QK_NOPE + QK_ROPE)), Q_LORA ** -0.5),
        "mla_w_dkv": nrm(ks[7], (nA, D, KV_LORA + QK_ROPE), D ** -0.5),
        "mla_kv_norm": 1.0 + nrm(ks[8], (nA, KV_LORA), 0.02),
        "mla_w_ukv": nrm(ks[9], (nA, KV_LORA, H * (QK_NOPE + V_HEAD)), KV_LORA ** -0.5),
        "mla_w_o": nrm(ks[10], (nA, H * V_HEAD, D), (H * V_HEAD) ** -0.5 * DEEPNORM_BETA),
        "fnet_w": nrm(ks[11], (nB, D, D), D ** -0.5 * DEEPNORM_BETA),
        "fnet_b": nrm(ks[12], (nB, D), 0.02),
        "ln_mix_g": 1.0 + nrm(ks[13], (DEPTH, D), 0.02),
        "ln_mix_b": nrm(ks[14], (DEPTH, D), 0.02),
        "router_w": nrm(ks[15], (DEPTH, D, E), D ** -0.5),
        "router_b": nrm(ks[16], (DEPTH, E), 0.01),
        "exp_w_gate": nrm(ks[17], (DEPTH, E, D, F), D ** -0.5),
        "exp_b_gate": nrm(ks[18], (DEPTH, E, F), 0.02),
        "exp_w_up": nrm(ks[19], (DEPTH, E, D, F), D ** -0.5),
        "exp_b_up": nrm(ks[20], (DEPTH, E, F), 0.02),
        "exp_w_down": nrm(ks[21], (DEPTH, E, F, D), F ** -0.5 * DEEPNORM_BETA),
        "exp_b_down": nrm(ks[22], (DEPTH, E, D), 0.02),
        "ln_moe_g": 1.0 + nrm(ks[23], (DEPTH, D), 0.02),
        "ln_moe_b": nrm(ks[24], (DEPTH, D), 0.02),
    }


def reference(x, positions, ln_in_g, ln_in_b, mla_w_dq, mla_q_norm, mla_w_uq, mla_w_dkv,
              mla_kv_norm, mla_w_ukv, mla_w_o, fnet_w, fnet_b, ln_mix_g, ln_mix_b,
              router_w, router_b, exp_w_gate, exp_b_gate, exp_w_up, exp_b_up,
              exp_w_down, exp_b_down, ln_moe_g, ln_moe_b):
    cos, sin = rope_tables(positions)
    h = layer_norm(x, ln_in_g, ln_in_b)
    for i in range(DEPTH):
        j = i // N_MIXERS
        if i % N_MIXERS == 0:
            m = mla_mixer(h, cos, sin, mla_w_dq[j], mla_q_norm[j], mla_w_uq[j], mla_w_dkv[j],
                          mla_kv_norm[j], mla_w_ukv[j], mla_w_o[j])
        else:
            m = fourier_mixer(h, fnet_w[j], fnet_b[j])
        h = layer_norm(DEEPNORM_ALPHA * h + m, ln_mix_g[i], ln_mix_b[i])
        f = moe(h, router_w[i], router_b[i], exp_w_gate[i], exp_b_gate[i], exp_w_up[i],
                exp_b_up[i], exp_w_down[i], exp_b_down[i])
        h = layer_norm(DEEPNORM_ALPHA * h + f, ln_moe_g[i], ln_moe_b[i])
    return h
```

```python
import functools
import math

import jax
import jax.numpy as jnp
from jax import lax
from jax.experimental import pallas as pl
from jax.experimental.pallas import tpu as pltpu

F32 = jnp.float32
BF16 = jnp.bfloat16

DEPTH = 4
N_HEADS = 16
Q_LORA = 512
KV_LORA = 512
QK_NOPE = 128
QK_ROPE = 64
V_HEAD = 128
ROPE_THETA = 10000.0
FNET_GROUPS = 4
N_EXPERTS = 32
TOP_K = 4
SWIGLU_LIMIT = 7.0
SWIGLU_ALPHA = 1.702
DEEPNORM_ALPHA = (2 * DEPTH) ** 0.25
LN_EPS = 1e-5
RMS_EPS = 1e-6

HEAD_PAD = 256
VMEM_LIMIT = 56 * 1024 * 1024
DMA_UNROLL = 8


def _cparams(sem):
    return pltpu.CompilerParams(dimension_semantics=sem, vmem_limit_bytes=VMEM_LIMIT)


def _layer_norm(y, g, b):
    mu = jnp.mean(y, axis=-1, keepdims=True)
    yc = y - mu
    var = jnp.mean(yc * yc, axis=-1, keepdims=True)
    return yc * lax.rsqrt(var + LN_EPS) * g + b


def _rms_norm(y, g):
    ms = jnp.mean(y * y, axis=-1, keepdims=True)
    return y * lax.rsqrt(ms + RMS_EPS) * g


def _ln_kernel(x_ref, g_ref, b_ref, o_ref):
    o_ref[...] = _layer_norm(x_ref[...], g_ref[...], b_ref[...])


def layer_norm_rows(x, g, b, tm=512):
    n, d = x.shape
    row = pl.BlockSpec((tm, d), lambda i: (i, 0))
    vec = pl.BlockSpec((1, d), lambda i: (0, 0))
    return pl.pallas_call(
        _ln_kernel,
        grid=(n // tm,),
        in_specs=[row, vec, vec],
        out_specs=row,
        out_shape=jax.ShapeDtypeStruct((n, d), F32),
        compiler_params=_cparams(("parallel",)),
        name="ln_in",
    )(x, g.reshape(1, d), b.reshape(1, d))


def _rope(x, c, s):
    return x * c + pltpu.roll(x, 64, 1) * s


def _q_proj_kernel(h_ref, c_ref, s_ref, wdq_ref, qn_ref, wn_ref, wp_ref, q_ref, *, n_heads, q_scale):
    x = h_ref[...].astype(BF16)
    qa = jnp.dot(x, wdq_ref[...], preferred_element_type=F32)
    cq = _rms_norm(qa, qn_ref[...]).astype(BF16)
    qn = jnp.dot(cq, wn_ref[...], preferred_element_type=F32) * q_scale
    qp = jnp.dot(cq, wp_ref[...], preferred_element_type=F32) * q_scale
    c = c_ref[...]
    s = s_ref[...]
    for hd in range(n_heads):
        sl = slice(hd * 128, (hd + 1) * 128)
        q_ref[hd, :, 0:128] = qn[:, sl].astype(BF16)
        q_ref[hd, :, 128:256] = _rope(qp[:, sl], c, s).astype(BF16)


def _kv_proj_kernel(h_ref, c_ref, s_ref, wdkv_ref, kvn_ref, wk_ref, wvt_ref, k_ref, vt_ref, *, n_heads, kv_lora):
    x = h_ref[...].astype(BF16)
    kva = jnp.dot(x, wdkv_ref[...], preferred_element_type=F32)
    ckv = _rms_norm(kva[:, :kv_lora], kvn_ref[...]).astype(BF16)
    kpe = _rope(kva[:, kv_lora:], c_ref[...], s_ref[...]).astype(BF16)
    kn = jnp.dot(ckv, wk_ref[...], preferred_element_type=F32)
    vt = lax.dot_general(wvt_ref[...], ckv, (((1,), (1,)), ((), ())),
                         preferred_element_type=F32)
    for hd in range(n_heads):
        sl = slice(hd * 128, (hd + 1) * 128)
        k_ref[hd, :, 0:128] = kn[:, sl].astype(BF16)
        k_ref[hd, :, 128:256] = kpe
        vt_ref[hd] = vt[sl, :].astype(BF16)


def mla_projections(h, rope_c, rope_s, w, batch, seq, tm):
    n, d = h.shape
    nh = N_HEADS
    nt = seq // tm
    q_scale = (QK_NOPE + QK_ROPE) ** -0.5 * math.log2(math.e)

    row = pl.BlockSpec((tm, d), lambda b, i: (b * nt + i, 0))
    tab = pl.BlockSpec((tm, 128), lambda b, i: (b * nt + i, 0))

    def full(a):
        return pl.BlockSpec(a.shape, lambda b, i: (0,) * a.ndim)

    head_out = pl.BlockSpec((None, nh, tm, HEAD_PAD), lambda b, i: (b, 0, i, 0))
    q = pl.pallas_call(
        functools.partial(_q_proj_kernel, n_heads=nh, q_scale=q_scale),
        grid=(batch, nt),
        in_specs=[row, tab, tab, full(w["wdq"]), full(w["qn"]), full(w["wuq_n"]), full(w["wuq_p"])],
        out_specs=head_out,
        out_shape=jax.ShapeDtypeStruct((batch, nh, seq, HEAD_PAD), BF16),
        compiler_params=_cparams(("parallel", "parallel")),
        name="mla_q_proj",
    )(h, rope_c, rope_s, w["wdq"], w["qn"], w["wuq_n"], w["wuq_p"])

    k, vt = pl.pallas_call(
        functools.partial(_kv_proj_kernel, n_heads=nh, kv_lora=KV_LORA),
        grid=(batch, nt),
        in_specs=[row, tab, tab, full(w["wdkv"]), full(w["kvn"]), full(w["wuk"]), full(w["wuvt"])],
        out_specs=[head_out,
                   pl.BlockSpec((None, nh, None, V_HEAD, tm), lambda b, i: (b, 0, i, 0, 0))],
        out_shape=[jax.ShapeDtypeStruct((batch, nh, seq, HEAD_PAD), BF16),
                   jax.ShapeDtypeStruct((batch, nh, nt, V_HEAD, tm), BF16)],
        compiler_params=_cparams(("parallel", "parallel")),
        name="mla_kv_proj",
    )(h, rope_c, rope_s, w["wdkv"], w["kvn"], w["wuk"], w["wuvt"])
    return q, k, vt


def _attn_kernel(q_ref, k_ref, vt_ref, o_ref, *, tq, tk):
    seq = q_ref.shape[0]
    nq = seq // tq
    nk = seq // tk

    def q_body(qi, carry):
        q0 = pl.multiple_of(qi * tq, tq)
        q = q_ref[pl.ds(q0, tq), :]

        def kv_body(j, st):
            m, l, acc = st
            k0 = pl.multiple_of(j * tk, tk)
            kc = k_ref[pl.ds(k0, tk), :]
            st_ = lax.dot_general(kc, q, (((1,), (1,)), ((), ())),
                                  preferred_element_type=F32)
            m_new = jnp.maximum(m, jnp.max(st_, axis=0, keepdims=True))
            alpha = jnp.exp2(m - m_new)
            p = jnp.exp2(st_ - m_new)
            l = alpha * l + jnp.sum(p, axis=0, keepdims=True)
            pv = jnp.dot(vt_ref[j], p.astype(BF16), preferred_element_type=F32)
            return m_new, l, alpha * acc + pv

        m0 = jnp.full((1, tq), -1e30, F32)
        l0 = jnp.zeros((1, tq), F32)
        a0 = jnp.zeros((V_HEAD, tq), F32)
        m, l, acc = lax.fori_loop(0, nk, kv_body, (m0, l0, a0))
        o_ref[pl.ds(q0, tq), :] = (acc / l).T.astype(BF16)
        return carry

    lax.fori_loop(0, nq, q_body, 0)


def attention(q, k, vt, tq):
    batch, nh, seq, _ = q.shape
    nt, tk = vt.shape[2], vt.shape[4]
    return pl.pallas_call(
        functools.partial(_attn_kernel, tq=tq, tk=tk),
        grid=(batch, nh),
        in_specs=[pl.BlockSpec((None, None, seq, HEAD_PAD), lambda b, h: (b, h, 0, 0)),
                  pl.BlockSpec((None, None, seq, HEAD_PAD), lambda b, h: (b, h, 0, 0)),
                  pl.BlockSpec((None, None, nt, V_HEAD, tk), lambda b, h: (b, h, 0, 0, 0))],
        out_specs=pl.BlockSpec((None, seq, V_HEAD), lambda b, h: (b, 0, h)),
        out_shape=jax.ShapeDtypeStruct((batch, seq, nh * V_HEAD), BF16),
        compiler_params=_cparams(("parallel", "parallel")),
        name="mla_attention",
    )(q, k, vt)


def _proj_ln_kernel(a_ref, w_ref, bias_ref, h_ref, g_ref, b_ref, o_ref):
    m = jnp.dot(a_ref[...], w_ref[...], preferred_element_type=F32) + bias_ref[...]
    o_ref[...] = _layer_norm(DEEPNORM_ALPHA * h_ref[...] + m, g_ref[...], b_ref[...])


def proj_residual_ln(a, w, bias, h, g, b, tm=512):
    n, d = h.shape
    kdim = a.shape[1]
    vec = pl.BlockSpec((1, d), lambda i: (0, 0))
    return pl.pallas_call(
        _proj_ln_kernel,
        grid=(n // tm,),
        in_specs=[pl.BlockSpec((tm, kdim), lambda i: (i, 0)),
                  pl.BlockSpec((kdim, d), lambda i: (0, 0)),
                  vec,
                  pl.BlockSpec((tm, d), lambda i: (i, 0)),
                  vec, vec],
        out_specs=pl.BlockSpec((tm, d), lambda i: (i, 0)),
        out_shape=jax.ShapeDtypeStruct((n, d), F32),
        compiler_params=_cparams(("parallel",)),
        name="proj_residual_ln",
    )(a, w, bias.reshape(1, d), h, g.reshape(1, d), b.reshape(1, d))


def _chan_dft_kernel(h_ref, t_ref, z_ref, *, groups, gd):
    t = t_ref[...]
    for g in range(groups):
        sl = slice(g * gd, (g + 1) * gd)
        xg = h_ref[:, sl].astype(BF16)
        zz = jnp.dot(xg, t, preferred_element_type=F32)
        z_ref[0, :, sl] = zz[:, :gd].astype(BF16)
        z_ref[1, :, sl] = zz[:, gd:].astype(BF16)


def channel_dft(h, tab, tm):
    n, d = h.shape
    gd = d // FNET_GROUPS
    return pl.pallas_call(
        functools.partial(_chan_dft_kernel, groups=FNET_GROUPS, gd=gd),
        grid=(n // tm,),
        in_specs=[pl.BlockSpec((tm, d), lambda i: (i, 0)),
                  pl.BlockSpec((gd, 2 * gd), lambda i: (0, 0))],
        out_specs=pl.BlockSpec((None, 2, tm, d), lambda i: (i, 0, 0, 0)),
        out_shape=jax.ShapeDtypeStruct((n // tm, 2, tm, d), BF16),
        compiler_params=_cparams(("parallel",)),
        name="fnet_channel_dft",
    )(h, tab)


def _pos_dft_kernel(t_ref, z_ref, o_ref, acc_ref, *, norm):
    k = pl.program_id(2)

    @pl.when(k == 0)
    def _():
        acc_ref[...] = jnp.zeros_like(acc_ref)

    acc_ref[...] += jnp.dot(t_ref[...], z_ref[...], preferred_element_type=F32)

    @pl.when(k == pl.num_programs(2) - 1)
    def _():
        o_ref[...] = (acc_ref[...] * norm).astype(BF16)


def position_dft(tab, z, norm, tm, tk):
    batch, k2, d = z.shape
    seq = tab.shape[0]
    return pl.pallas_call(
        functools.partial(_pos_dft_kernel, norm=norm),
        grid=(batch, seq // tm, k2 // tk),
        in_specs=[pl.BlockSpec((tm, tk), lambda b, i, k: (i, k)),
                  pl.BlockSpec((None, tk, d), lambda b, i, k: (b, k, 0))],
        out_specs=pl.BlockSpec((None, tm, d), lambda b, i, k: (b, i, 0)),
        out_shape=jax.ShapeDtypeStruct((batch, seq, d), BF16),
        scratch_shapes=[pltpu.VMEM((tm, d), F32)],
        compiler_params=_cparams(("parallel", "parallel", "arbitrary")),
        name="fnet_position_dft",
    )(tab, z)


def _split_bf16(x):
    hi = x.astype(BF16)
    lo = (x - hi.astype(F32)).astype(BF16)
    return hi, lo


def _router_kernel(h_ref, wt_ref, b_ref, idx_ref, gate_ref, *, top_k):
    nt = (((1,), (1,)), ((), ()))
    x_hi, x_lo = _split_bf16(h_ref[...])
    w_hi, w_lo = _split_bf16(wt_ref[...])
    logits = (lax.dot_general(w_hi, x_hi, nt, preferred_element_type=F32)
              + lax.dot_general(w_hi, x_lo, nt, preferred_element_type=F32)
              + lax.dot_general(w_lo, x_hi, nt, preferred_element_type=F32)
              + b_ref[...])
    ne = logits.shape[0]
    eid = lax.broadcasted_iota(jnp.int32, logits.shape, 0)
    vals, idxs = [], []
    cur = logits
    for _ in range(top_k):
        mx = jnp.max(cur, axis=0, keepdims=True)
        ix = jnp.min(jnp.where(cur == mx, eid, ne), axis=0, keepdims=True)
        vals.append(mx)
        idxs.append(ix)
        cur = jnp.where(eid == ix, -jnp.inf, cur)
    ex = [jnp.exp(v - vals[0]) for v in vals]
    tot = ex[0]
    for e in ex[1:]:
        tot = tot + e
    for kk in range(top_k):
        idx_ref[kk:kk + 1, :] = idxs[kk]
        gate_ref[kk:kk + 1, :] = ex[kk] / tot


def router(h, w_t, bias, tm=512):
    n, d = h.shape
    ne = w_t.shape[0]
    return pl.pallas_call(
        functools.partial(_router_kernel, top_k=TOP_K),
        grid=(n // tm,),
        in_specs=[pl.BlockSpec((tm, d), lambda i: (i, 0)),
                  pl.BlockSpec((ne, d), lambda i: (0, 0)),
                  pl.BlockSpec((ne, 1), lambda i: (0, 0))],
        out_specs=[pl.BlockSpec((TOP_K, tm), lambda i: (0, i)),
                   pl.BlockSpec((TOP_K, tm), lambda i: (0, i))],
        out_shape=[jax.ShapeDtypeStruct((TOP_K, n), jnp.int32),
                   jax.ShapeDtypeStruct((TOP_K, n), F32)],
        compiler_params=_cparams(("parallel",)),
        name="moe_router",
    )(h, w_t, bias.reshape(ne, 1))


def _swiglu(hg, hu):
    g = jnp.minimum(hg, SWIGLU_LIMIT)
    l = jnp.clip(hu, -SWIGLU_LIMIT, SWIGLU_LIMIT)
    return g * (1.0 / (1.0 + jnp.exp(-SWIGLU_ALPHA * g))) * (l + 1.0)


def _expert_kernel(ue_ref, ustart_ref, ulen_ref, order_ref,
                   h_hbm, wg_ref, bg_ref, wu_ref, bu_ref, wd_ref, bd_ref,
                   ys_hbm,
                   stage, xb, wgb, wub, wdb, act, ybuf, sem_g, sem_s,
                   *, rows, sub, nf, tf, n_tok, n_assign):
    u = pl.program_id(0)
    s = pl.program_id(1)
    ulen = ulen_ref[u]
    start = ustart_ref[u]
    nsub = rows // sub
    n_pad = ((ulen + sub - 1) // sub) * sub

    def assignment(i):
        return order_ref[jnp.minimum(start + i, n_assign - 1)]

    def row_in(i):
        tok = lax.shift_right_logical(assignment(i), 2)
        return pltpu.make_async_copy(h_hbm.at[pl.ds(tok, 1), :], stage.at[pl.ds(i, 1), :], sem_g)

    def row_out(i):
        a = assignment(i)
        dest = (a & (TOP_K - 1)) * n_tok + lax.shift_right_logical(a, 2)
        return pltpu.make_async_copy(ybuf.at[pl.ds(i, 1), :], ys_hbm.at[pl.ds(dest, 1), :], sem_s)

    def for_rows(n, fn):
        def group(gi, c):
            for jj in range(DMA_UNROLL):
                fn(gi * DMA_UNROLL + jj)
            return c
        n_groups = n // DMA_UNROLL
        lax.fori_loop(0, n_groups, group, 0)

        def single(i, c):
            fn(i)
            return c
        lax.fori_loop(n_groups * DMA_UNROLL, n, single, 0)

    @pl.when(jnp.logical_and(s == 0, ulen > 0))
    def _gather():
        for_rows(n_pad, lambda i: row_in(i).start())
        for_rows(n_pad, lambda i: row_in(i).wait())
        for sb in range(nsub):
            @pl.when(sb * sub < ulen)
            def _():
                r = pl.ds(sb * sub, sub)
                xb[r, :] = stage[r, :].astype(BF16)

    @pl.when(jnp.logical_and(s < nf, ulen > 0))
    def _gate_up():
        wgb[...] = wg_ref[...].astype(BF16)
        wub[...] = wu_ref[...].astype(BF16)
        for sb in range(nsub):
            @pl.when(sb * sub < ulen)
            def _():
                r = pl.ds(sb * sub, sub)
                x = xb[r, :]
                hg = jnp.dot(x, wgb[...], preferred_element_type=F32) + bg_ref[...]
                hu = jnp.dot(x, wub[...], preferred_element_type=F32) + bu_ref[...]
                act[s, r, :] = _swiglu(hg, hu).astype(BF16)

    @pl.when(jnp.logical_and(s == nf, ulen > 0))
    def _down():
        wdb[...] = wd_ref[...].astype(BF16)
        for sb in range(nsub):
            @pl.when(sb * sub < ulen)
            def _():
                r = pl.ds(sb * sub, sub)
                y = bd_ref[...] + jnp.dot(act[0, r, :], wdb[0:tf, :], preferred_element_type=F32)
                for f in range(1, nf):
                    y = y + jnp.dot(act[f, r, :], wdb[f * tf:(f + 1) * tf, :], preferred_element_type=F32)
                ybuf[r, :] = y

        for_rows(ulen, lambda i: row_out(i).start())
        for_rows(ulen, lambda i: row_out(i).wait())


def moe_experts(h, ue, ustart, ulen, order, wg, bg, wu, bu, wd, bd, rows, sub, tf):
    n_tok, d = h.shape
    ne, _, fdim = wg.shape
    nf = fdim // tf
    n_assign = n_tok * TOP_K
    n_units = ue.shape[0]
    last_f = nf - 1

    def f_idx(u, s, ulen_ref):
        return jnp.where(ulen_ref[u] > 0, jnp.minimum(s, last_f), last_f)

    w_in = pl.BlockSpec((None, d, tf), lambda u, s, ue, us, ul, od: (ue[u], 0, f_idx(u, s, ul)))
    b_in = pl.BlockSpec((None, 1, tf), lambda u, s, ue, us, ul, od: (ue[u], 0, f_idx(u, s, ul)))
    w_dn = pl.BlockSpec((None, fdim, d), lambda u, s, ue, us, ul, od: (ue[u], 0, 0))
    b_dn = pl.BlockSpec((None, 1, d), lambda u, s, ue, us, ul, od: (ue[u], 0, 0))
    any_spec = pl.BlockSpec(memory_space=pl.ANY)

    grid_spec = pltpu.PrefetchScalarGridSpec(
        num_scalar_prefetch=4,
        grid=(n_units, nf + 1),
        in_specs=[any_spec, w_in, b_in, w_in, b_in, w_dn, b_dn],
        out_specs=any_spec,
        scratch_shapes=[
            pltpu.VMEM((rows, d), F32),
            pltpu.VMEM((rows, d), BF16),
            pltpu.VMEM((d, tf), BF16),
            pltpu.VMEM((d, tf), BF16),
            pltpu.VMEM((fdim, d), BF16),
            pltpu.VMEM((nf, rows, tf), BF16),
            pltpu.VMEM((rows, d), F32),
            pltpu.SemaphoreType.DMA(()),
            pltpu.SemaphoreType.DMA(()),
        ],
    )
    return pl.pallas_call(
        functools.partial(_expert_kernel, rows=rows, sub=sub, nf=nf, tf=tf,
                          n_tok=n_tok, n_assign=n_assign),
        grid_spec=grid_spec,
        out_shape=jax.ShapeDtypeStruct((n_assign, d), F32),
        compiler_params=_cparams(("arbitrary", "arbitrary")),
        name="moe_experts",
    )(ue, ustart, ulen, order, h, wg, bg.reshape(ne, 1, fdim), wu, bu.reshape(ne, 1, fdim),
      wd, bd.reshape(ne, 1, d))


def routing_tables(idx_t, rows):
    k, n = idx_t.shape
    n_assign = k * n
    flat_e = idx_t.T.reshape(-1)
    order = jnp.argsort(flat_e, stable=True).astype(jnp.int32)
    counts = jnp.sum(flat_e[:, None] == jnp.arange(N_EXPERTS, dtype=jnp.int32)[None, :],
                     axis=0, dtype=jnp.int32)
    cstart = jnp.cumsum(counts) - counts
    nu = (counts + rows - 1) // rows
    ucum = jnp.cumsum(nu)
    n_units = n_assign // rows + N_EXPERTS
    uid = jnp.arange(n_units, dtype=jnp.int32)
    valid = uid < ucum[-1]
    e_u = jnp.minimum(jnp.searchsorted(ucum, uid, side="right"), N_EXPERTS - 1).astype(jnp.int32)
    e_last = e_u[jnp.maximum(ucum[-1] - 1, 0)]
    e_u = jnp.where(valid, e_u, e_last)
    j = uid - (ucum[e_u] - nu[e_u])
    ustart = jnp.where(valid, cstart[e_u] + j * rows, 0).astype(jnp.int32)
    ulen = jnp.where(valid, jnp.minimum(rows, counts[e_u] - j * rows), 0).astype(jnp.int32)
    return e_u, ustart, ulen, order


def _combine_ln_kernel(y0, y1, y2, y3, gt_ref, h_ref, g_ref, b_ref, o_ref):
    gt = gt_ref[...]
    f = (y0[...] * gt[:, 0:1] + y1[...] * gt[:, 1:2]
         + y2[...] * gt[:, 2:3] + y3[...] * gt[:, 3:4])
    o_ref[...] = _layer_norm(DEEPNORM_ALPHA * h_ref[...] + f, g_ref[...], b_ref[...])


def combine_residual_ln(ys, gates, h, g, b, tm=256):
    n, d = h.shape
    nb = n // tm
    vec = pl.BlockSpec((1, d), lambda i: (0, 0))

    def slot(kk):
        return pl.BlockSpec((tm, d), lambda i: (kk * nb + i, 0))

    return pl.pallas_call(
        _combine_ln_kernel,
        grid=(nb,),
        in_specs=[slot(0), slot(1), slot(2), slot(3),
                  pl.BlockSpec((tm, TOP_K), lambda i: (i, 0)),
                  pl.BlockSpec((tm, d), lambda i: (i, 0)), vec, vec],
        out_specs=pl.BlockSpec((tm, d), lambda i: (i, 0)),
        out_shape=jax.ShapeDtypeStruct((n, d), F32),
        compiler_params=_cparams(("parallel",)),
        name="moe_combine_ln",
    )(ys, ys, ys, ys, gates, h, g.reshape(1, d), b.reshape(1, d))


def _prep_mla(w_dq, q_norm, w_uq, w_dkv, kv_norm, w_ukv, w_o):
    nh = N_HEADS
    d = w_dq.shape[0]
    wuq = w_uq.reshape(Q_LORA, nh, QK_NOPE + QK_ROPE)
    pe = wuq[:, :, QK_NOPE:]
    z = jnp.zeros((Q_LORA, nh, QK_ROPE // 2), F32)
    wuq_p = jnp.concatenate([pe[:, :, 0::2], z, pe[:, :, 1::2], z], axis=-1)
    kpe = w_dkv[:, KV_LORA:]
    zk = jnp.zeros((d, QK_ROPE // 2), F32)
    wdkv = jnp.concatenate([w_dkv[:, :KV_LORA], kpe[:, 0::2], zk, kpe[:, 1::2], zk], axis=1)
    wukv = w_ukv.reshape(KV_LORA, nh, QK_NOPE + V_HEAD)
    return {
        "wdq": w_dq.astype(BF16),
        "qn": q_norm.reshape(1, Q_LORA),
        "wuq_n": wuq[:, :, :QK_NOPE].reshape(Q_LORA, nh * 128).astype(BF16),
        "wuq_p": wuq_p.reshape(Q_LORA, nh * 128).astype(BF16),
        "wdkv": wdkv.astype(BF16),
        "kvn": kv_norm.reshape(1, KV_LORA),
        "wuk": wukv[:, :, :QK_NOPE].reshape(KV_LORA, nh * 128).astype(BF16),
        "wuvt": wukv[:, :, QK_NOPE:].reshape(KV_LORA, nh * V_HEAD).T.astype(BF16),
        "wo": w_o.astype(BF16),
    }


def _rope_tables(positions):
    inv_freq = 1.0 / (ROPE_THETA ** (jnp.arange(0, QK_ROPE, 2, dtype=F32) / QK_ROPE))
    ang = positions.astype(F32).reshape(-1)[:, None] * inv_freq
    c, s = jnp.cos(ang), jnp.sin(ang)
    z = jnp.zeros_like(c)
    return jnp.concatenate([c, z, c, z], axis=1), jnp.concatenate([-s, z, s, z], axis=1)


def _dft_tables(seq, gd, tk):
    def cs(n):
        i = jnp.arange(n, dtype=jnp.int32)
        ang = ((i[:, None] * i[None, :]) % n).astype(F32) * (2.0 * math.pi / n)
        return jnp.cos(ang), jnp.sin(ang)

    cc, sc = cs(gd)
    chan = jnp.concatenate([cc, sc], axis=1).astype(BF16)
    cp, sp = cs(seq)
    nt = seq // tk
    pos = jnp.stack([cp.reshape(seq, nt, tk), -sp.reshape(seq, nt, tk)], axis=2)
    return chan, pos.reshape(seq, 2 * seq).astype(BF16)


def kernel(x, positions, ln_in_g, ln_in_b, mla_w_dq, mla_q_norm, mla_w_uq, mla_w_dkv, mla_kv_norm,
           mla_w_ukv, mla_w_o, fnet_w, fnet_b, ln_mix_g, ln_mix_b, router_w, router_b, exp_w_gate,
           exp_b_gate, exp_w_up, exp_b_up, exp_w_down, exp_b_down, ln_moe_g, ln_moe_b):
    batch, seq, d = x.shape
    n = batch * seq
    t_seq = 512
    moe_rows, moe_sub, moe_tf = 512, 256, 256

    rope_c, rope_s = _rope_tables(positions)
    gd = d // FNET_GROUPS
    chan_tab, pos_tab = _dft_tables(seq, gd, t_seq)
    dft_norm = 1.0 / math.sqrt(seq * gd)
    zero_bias = jnp.zeros((d,), F32)

    h = layer_norm_rows(x.reshape(n, d), ln_in_g, ln_in_b)
    for i in range(DEPTH):
        j = i // 2
        if i % 2 == 0:
            w = _prep_mla(mla_w_dq[j], mla_q_norm[j], mla_w_uq[j], mla_w_dkv[j], mla_kv_norm[j],
                          mla_w_ukv[j], mla_w_o[j])
            q, k, vt = mla_projections(h, rope_c, rope_s, w, batch, seq, t_seq)
            o = attention(q, k, vt, tq=512)
            h = proj_residual_ln(o.reshape(n, N_HEADS * V_HEAD), w["wo"], zero_bias, h,
                                 ln_mix_g[i], ln_mix_b[i])
        else:
            z = channel_dft(h, chan_tab, t_seq).reshape(batch, 2 * seq, d)
            mixed = position_dft(pos_tab, z, dft_norm, tm=1024, tk=2 * t_seq)
            h = proj_residual_ln(mixed.reshape(n, d), fnet_w[j].astype(BF16), fnet_b[j], h,
                                 ln_mix_g[i], ln_mix_b[i])
        idx_t, gate_t = router(h, router_w[i].T, router_b[i])
        ue, ustart, ulen, order = routing_tables(idx_t, moe_rows)
        ys = moe_experts(h, ue, ustart, ulen, order, exp_w_gate[i], exp_b_gate[i], exp_w_up[i],
                         exp_b_up[i], exp_w_down[i], exp_b_down[i], moe_rows, moe_sub, moe_tf)
        h = combine_residual_ln(ys, gate_t.T, h, ln_moe_g[i], ln_moe_b[i])
    return h.reshape(batch, seq, d)
```

```python
import functools
import math

import jax
import jax.numpy as jnp
from jax import lax
from jax.experimental import pallas as pl
from jax.experimental.pallas import tpu as pltpu

F32 = jnp.float32
BF16 = jnp.bfloat16

DEPTH = 4
N_HEADS = 16
Q_LORA = 512
KV_LORA = 512
QK_NOPE = 128
QK_ROPE = 64
V_HEAD = 128
ROPE_THETA = 10000.0
FNET_GROUPS = 4
N_EXPERTS = 32
TOP_K = 4
SWIGLU_LIMIT = 7.0
SWIGLU_ALPHA = 1.702
DEEPNORM_ALPHA = (2 * DEPTH) ** 0.25
LN_EPS = 1e-5
RMS_EPS = 1e-6

HEAD_PAD = 256
VMEM_LIMIT = 56 * 1024 * 1024
DMA_UNROLL = 16


def _cparams(sem):
    return pltpu.CompilerParams(dimension_semantics=sem, vmem_limit_bytes=VMEM_LIMIT)


def _layer_norm(y, g, b):
    mu = jnp.mean(y, axis=-1, keepdims=True)
    yc = y - mu
    var = jnp.mean(yc * yc, axis=-1, keepdims=True)
    return yc * lax.rsqrt(var + LN_EPS) * g + b


def _rms_norm(y, g):
    ms = jnp.mean(y * y, axis=-1, keepdims=True)
    return y * lax.rsqrt(ms + RMS_EPS) * g


def _ln_kernel(x_ref, g_ref, b_ref, o_ref):
    o_ref[...] = _layer_norm(x_ref[...], g_ref[...], b_ref[...])


def layer_norm_rows(x, g, b, tm=512):
    n, d = x.shape
    row = pl.BlockSpec((tm, d), lambda i: (i, 0))
    vec = pl.BlockSpec((1, d), lambda i: (0, 0))
    return pl.pallas_call(
        _ln_kernel,
        grid=(n // tm,),
        in_specs=[row, vec, vec],
        out_specs=row,
        out_shape=jax.ShapeDtypeStruct((n, d), F32),
        compiler_params=_cparams(("parallel",)),
        name="ln_in",
    )(x, g.reshape(1, d), b.reshape(1, d))


def _rope(x, c, s):
    return x * c + pltpu.roll(x, 64, 1) * s


def _q_proj_kernel(h_ref, c_ref, s_ref, wdq_ref, qn_ref, wn_ref, wp_ref, q_ref, *, n_heads, q_scale):
    x = h_ref[...].astype(BF16)
    qa = jnp.dot(x, wdq_ref[...], preferred_element_type=F32)
    cq = _rms_norm(qa, qn_ref[...]).astype(BF16)
    qn = jnp.dot(cq, wn_ref[...], preferred_element_type=F32) * q_scale
    qp = jnp.dot(cq, wp_ref[...], preferred_element_type=F32) * q_scale
    c = c_ref[...]
    s = s_ref[...]
    for hd in range(n_heads):
        sl = slice(hd * 128, (hd + 1) * 128)
        q_ref[hd, :, 0:128] = qn[:, sl].astype(BF16)
        q_ref[hd, :, 128:256] = _rope(qp[:, sl], c, s).astype(BF16)


def _kv_proj_kernel(h_ref, c_ref, s_ref, wdkv_ref, kvn_ref, wk_ref, wvt_ref, k_ref, vt_ref, *, n_heads, kv_lora):
    x = h_ref[...].astype(BF16)
    kva = jnp.dot(x, wdkv_ref[...], preferred_element_type=F32)
    ckv = _rms_norm(kva[:, :kv_lora], kvn_ref[...]).astype(BF16)
    kpe = _rope(kva[:, kv_lora:], c_ref[...], s_ref[...]).astype(BF16)
    kn = jnp.dot(ckv, wk_ref[...], preferred_element_type=F32)
    vt = lax.dot_general(wvt_ref[...], ckv, (((1,), (1,)), ((), ())),
                         preferred_element_type=F32)
    for hd in range(n_heads):
        sl = slice(hd * 128, (hd + 1) * 128)
        k_ref[hd, :, 0:128] = kn[:, sl].astype(BF16)
        k_ref[hd, :, 128:256] = kpe
        vt_ref[hd] = vt[sl, :].astype(BF16)


def mla_projections(h, rope_c, rope_s, w, batch, seq, tm):
    n, d = h.shape
    nh = N_HEADS
    nt = seq // tm
    q_scale = (QK_NOPE + QK_ROPE) ** -0.5 * math.log2(math.e)

    row = pl.BlockSpec((tm, d), lambda b, i: (b * nt + i, 0))
    tab = pl.BlockSpec((tm, 128), lambda b, i: (b * nt + i, 0))

    def full(a):
        return pl.BlockSpec(a.shape, lambda b, i: (0,) * a.ndim)

    head_out = pl.BlockSpec((None, nh, tm, HEAD_PAD), lambda b, i: (b, 0, i, 0))
    q = pl.pallas_call(
        functools.partial(_q_proj_kernel, n_heads=nh, q_scale=q_scale),
        grid=(batch, nt),
        in_specs=[row, tab, tab, full(w["wdq"]), full(w["qn"]), full(w["wuq_n"]), full(w["wuq_p"])],
        out_specs=head_out,
        out_shape=jax.ShapeDtypeStruct((batch, nh, seq, HEAD_PAD), BF16),
        compiler_params=_cparams(("parallel", "parallel")),
        name="mla_q_proj",
    )(h, rope_c, rope_s, w["wdq"], w["qn"], w["wuq_n"], w["wuq_p"])

    k, vt = pl.pallas_call(
        functools.partial(_kv_proj_kernel, n_heads=nh, kv_lora=KV_LORA),
        grid=(batch, nt),
        in_specs=[row, tab, tab, full(w["wdkv"]), full(w["kvn"]), full(w["wuk"]), full(w["wuvt"])],
        out_specs=[head_out,
                   pl.BlockSpec((None, nh, None, V_HEAD, tm), lambda b, i: (b, 0, i, 0, 0))],
        out_shape=[jax.ShapeDtypeStruct((batch, nh, seq, HEAD_PAD), BF16),
                   jax.ShapeDtypeStruct((batch, nh, nt, V_HEAD, tm), BF16)],
        compiler_params=_cparams(("parallel", "parallel")),
        name="mla_kv_proj",
    )(h, rope_c, rope_s, w["wdkv"], w["kvn"], w["wuk"], w["wuvt"])
    return q, k, vt


def _attn_kernel(q_ref, k_ref, vt_ref, o_ref, *, tq, tk, n_chain, kv_unroll):
    seq = q_ref.shape[0]
    nq = seq // tq
    nk = seq // tk
    tc = tq // n_chain

    def q_body(qi, carry):
        q0 = pl.multiple_of(qi * tq, tq)
        qs = [q_ref[pl.ds(q0 + c * tc, tc), :] for c in range(n_chain)]

        def kv_body(jj, st):
            st = list(st)
            for uu in range(kv_unroll):
                j = jj * kv_unroll + uu
                k0 = pl.multiple_of(j * tk, tk)
                kc = k_ref[pl.ds(k0, tk), :]
                vtc = vt_ref[j]
                for c in range(n_chain):
                    m, l, acc = st[c]
                    sc = lax.dot_general(kc, qs[c], (((1,), (1,)), ((), ())),
                                         preferred_element_type=F32)
                    m_new = jnp.maximum(m, jnp.max(sc, axis=0, keepdims=True))
                    alpha = jnp.exp2(m - m_new)
                    p = jnp.exp2(sc - m_new)
                    l = alpha * l + jnp.sum(p, axis=0, keepdims=True)
                    pv = jnp.dot(vtc, p.astype(BF16), preferred_element_type=F32)
                    st[c] = (m_new, l, alpha * acc + pv)
            return tuple(st)

        init = tuple((jnp.full((1, tc), -1e30, F32), jnp.zeros((1, tc), F32),
                      jnp.zeros((V_HEAD, tc), F32)) for _ in range(n_chain))
        st = lax.fori_loop(0, nk // kv_unroll, kv_body, init)
        for c in range(n_chain):
            _, l, acc = st[c]
            o_ref[pl.ds(q0 + c * tc, tc), :] = (acc / l).T.astype(BF16)
        return carry

    lax.fori_loop(0, nq, q_body, 0)


def attention(q, k, vt, tq):
    batch, nh, seq, _ = q.shape
    nt, tk = vt.shape[2], vt.shape[4]
    return pl.pallas_call(
        functools.partial(_attn_kernel, tq=tq, tk=tk, n_chain=1, kv_unroll=1),
        grid=(batch, nh),
        in_specs=[pl.BlockSpec((None, None, seq, HEAD_PAD), lambda b, h: (b, h, 0, 0)),
                  pl.BlockSpec((None, None, seq, HEAD_PAD), lambda b, h: (b, h, 0, 0)),
                  pl.BlockSpec((None, None, nt, V_HEAD, tk), lambda b, h: (b, h, 0, 0, 0))],
        out_specs=pl.BlockSpec((None, seq, V_HEAD), lambda b, h: (b, 0, h)),
        out_shape=jax.ShapeDtypeStruct((batch, seq, nh * V_HEAD), BF16),
        compiler_params=_cparams(("parallel", "parallel")),
        name="mla_attention",
    )(q, k, vt)


def _proj_ln_kernel(a_ref, w_ref, bias_ref, h_ref, g_ref, b_ref, o_ref):
    m = jnp.dot(a_ref[...], w_ref[...], preferred_element_type=F32) + bias_ref[...]
    o_ref[...] = _layer_norm(DEEPNORM_ALPHA * h_ref[...] + m, g_ref[...], b_ref[...])


def proj_residual_ln(a, w, bias, h, g, b, tm=512):
    n, d = h.shape
    kdim = a.shape[1]
    vec = pl.BlockSpec((1, d), lambda i: (0, 0))
    return pl.pallas_call(
        _proj_ln_kernel,
        grid=(n // tm,),
        in_specs=[pl.BlockSpec((tm, kdim), lambda i: (i, 0)),
                  pl.BlockSpec((kdim, d), lambda i: (0, 0)),
                  vec,
                  pl.BlockSpec((tm, d), lambda i: (i, 0)),
                  vec, vec],
        out_specs=pl.BlockSpec((tm, d), lambda i: (i, 0)),
        out_shape=jax.ShapeDtypeStruct((n, d), F32),
        compiler_params=_cparams(("parallel",)),
        name="proj_residual_ln",
    )(a, w, bias.reshape(1, d), h, g.reshape(1, d), b.reshape(1, d))


def _chan_dft_kernel(h_ref, t_ref, z_ref, *, groups, gd):
    t = t_ref[...]
    for g in range(groups):
        sl = slice(g * gd, (g + 1) * gd)
        xg = h_ref[:, sl].astype(BF16)
        zz = jnp.dot(xg, t, preferred_element_type=F32)
        z_ref[0, :, sl] = zz[:, :gd].astype(BF16)
        z_ref[1, :, sl] = zz[:, gd:].astype(BF16)


def channel_dft(h, tab, tm):
    n, d = h.shape
    gd = d // FNET_GROUPS
    return pl.pallas_call(
        functools.partial(_chan_dft_kernel, groups=FNET_GROUPS, gd=gd),
        grid=(n // tm,),
        in_specs=[pl.BlockSpec((tm, d), lambda i: (i, 0)),
                  pl.BlockSpec((gd, 2 * gd), lambda i: (0, 0))],
        out_specs=pl.BlockSpec((None, 2, tm, d), lambda i: (i, 0, 0, 0)),
        out_shape=jax.ShapeDtypeStruct((n // tm, 2, tm, d), BF16),
        compiler_params=_cparams(("parallel",)),
        name="fnet_channel_dft",
    )(h, tab)


def _pos_dft_kernel(t_ref, z_ref, o_ref, acc_ref, *, norm):
    k = pl.program_id(2)

    @pl.when(k == 0)
    def _():
        acc_ref[...] = jnp.zeros_like(acc_ref)

    acc_ref[...] += jnp.dot(t_ref[...], z_ref[...], preferred_element_type=F32)

    @pl.when(k == pl.num_programs(2) - 1)
    def _():
        o_ref[...] = (acc_ref[...] * norm).astype(BF16)


def position_dft(tab, z, norm, tm, tk):
    batch, k2, d = z.shape
    seq = tab.shape[0]
    return pl.pallas_call(
        functools.partial(_pos_dft_kernel, norm=norm),
        grid=(batch, seq // tm, k2 // tk),
        in_specs=[pl.BlockSpec((tm, tk), lambda b, i, k: (i, k)),
                  pl.BlockSpec((None, tk, d), lambda b, i, k: (b, k, 0))],
        out_specs=pl.BlockSpec((None, tm, d), lambda b, i, k: (b, i, 0)),
        out_shape=jax.ShapeDtypeStruct((batch, seq, d), BF16),
        scratch_shapes=[pltpu.VMEM((tm, d), F32)],
        compiler_params=_cparams(("parallel", "parallel", "arbitrary")),
        name="fnet_position_dft",
    )(tab, z)


def _split_bf16(x):
    hi = x.astype(BF16)
    lo = (x - hi.astype(F32)).astype(BF16)
    return hi, lo


def _router_kernel(h_ref, wt_ref, b_ref, idx_ref, gate_ref, *, top_k):
    nt = (((1,), (1,)), ((), ()))
    x_hi, x_lo = _split_bf16(h_ref[...])
    w_hi, w_lo = _split_bf16(wt_ref[...])
    logits = (lax.dot_general(w_hi, x_hi, nt, preferred_element_type=F32)
              + lax.dot_general(w_hi, x_lo, nt, preferred_element_type=F32)
              + lax.dot_general(w_lo, x_hi, nt, preferred_element_type=F32)
              + b_ref[...])
    ne = logits.shape[0]
    eid = lax.broadcasted_iota(jnp.int32, logits.shape, 0)
    vals, idxs = [], []
    cur = logits
    for _ in range(top_k):
        mx = jnp.max(cur, axis=0, keepdims=True)
        ix = jnp.min(jnp.where(cur == mx, eid, ne), axis=0, keepdims=True)
        vals.append(mx)
        idxs.append(ix)
        cur = jnp.where(eid == ix, -jnp.inf, cur)
    ex = [jnp.exp(v - vals[0]) for v in vals]
    tot = ex[0]
    for e in ex[1:]:
        tot = tot + e
    for kk in range(top_k):
        idx_ref[kk:kk + 1, :] = idxs[kk]
        gate_ref[kk:kk + 1, :] = ex[kk] / tot


def router(h, w_t, bias, tm=512):
    n, d = h.shape
    ne = w_t.shape[0]
    return pl.pallas_call(
        functools.partial(_router_kernel, top_k=TOP_K),
        grid=(n // tm,),
        in_specs=[pl.BlockSpec((tm, d), lambda i: (i, 0)),
                  pl.BlockSpec((ne, d), lambda i: (0, 0)),
                  pl.BlockSpec((ne, 1), lambda i: (0, 0))],
        out_specs=[pl.BlockSpec((TOP_K, tm), lambda i: (0, i)),
                   pl.BlockSpec((TOP_K, tm), lambda i: (0, i))],
        out_shape=[jax.ShapeDtypeStruct((TOP_K, n), jnp.int32),
                   jax.ShapeDtypeStruct((TOP_K, n), F32)],
        compiler_params=_cparams(("parallel",)),
        name="moe_router",
    )(h, w_t, bias.reshape(ne, 1))


def _swiglu(hg, hu):
    g = jnp.minimum(hg, SWIGLU_LIMIT)
    l = jnp.clip(hu, -SWIGLU_LIMIT, SWIGLU_LIMIT)
    return g * (1.0 / (1.0 + jnp.exp(-SWIGLU_ALPHA * g))) * (l + 1.0)


def _expert_kernel(ue_ref, ustart_ref, ulen_ref, order_ref,
                   h_hbm, wg_ref, bg_ref, wu_ref, bu_ref, wd_ref, bd_ref,
                   ys_hbm,
                   stage, xb, wgb, wub, wdb, act, ybuf, sem_g, sem_s,
                   *, rows, sub, nf, tf, n_tok):
    u = pl.program_id(0)
    s = pl.program_id(1)
    ulen = ulen_ref[u]
    nsub = rows // sub

    sub_shift = sub.bit_length() - 1
    grp_shift = DMA_UNROLL.bit_length() - 1

    def padded(n):
        return lax.shift_left(lax.shift_right_logical(n + (sub - 1), sub_shift), sub_shift)

    def row_in(base, i):
        tok = lax.shift_right_logical(order_ref[base + i], 2)
        return pltpu.make_async_copy(h_hbm.at[pl.ds(tok, 1), :], stage.at[pl.ds(i, 1), :], sem_g)

    def row_out(base, i):
        a = order_ref[base + i]
        dest = (a & (TOP_K - 1)) * n_tok + lax.shift_right_logical(a, 2)
        return pltpu.make_async_copy(ybuf.at[pl.ds(i, 1), :], ys_hbm.at[pl.ds(dest, 1), :], sem_s)

    def for_rows(n, fn):
        full = lax.shift_right_logical(n, grp_shift)
        for g in range(rows // DMA_UNROLL):
            @pl.when(g < full)
            def _():
                for jj in range(DMA_UNROLL):
                    fn(g * DMA_UNROLL + jj)

        def single(i, c):
            fn(i)
            return c
        lax.fori_loop(lax.shift_left(full, grp_shift), n, single, 0)

    @pl.when(jnp.logical_and(jnp.logical_and(u == 0, s == 0), ulen > 0))
    def _first_gather():
        base = ustart_ref[0]

        def single(i, c):
            row_in(base, i).start()
            return c
        lax.fori_loop(0, padded(ulen), single, 0)

    @pl.when(jnp.logical_and(s == 0, ulen > 0))
    def _await_rows():
        base = ustart_ref[u]
        for_rows(padded(ulen), lambda i: row_in(base, i).wait())
        for sb in range(nsub):
            @pl.when(sb * sub < ulen)
            def _():
                r = pl.ds(sb * sub, sub)
                xb[r, :] = stage[r, :].astype(BF16)

    @pl.when(jnp.logical_and(s < nf, ulen > 0))
    def _gate_up():
        wgb[...] = wg_ref[...].astype(BF16)
        wub[...] = wu_ref[...].astype(BF16)
        for sb in range(nsub):
            @pl.when(sb * sub < ulen)
            def _():
                r = pl.ds(sb * sub, sub)
                x = xb[r, :]
                hg = jnp.dot(x, wgb[...], preferred_element_type=F32) + bg_ref[...]
                hu = jnp.dot(x, wub[...], preferred_element_type=F32) + bu_ref[...]
                act[s, r, :] = _swiglu(hg, hu).astype(BF16)

    @pl.when(jnp.logical_and(s == nf, ulen > 0))
    def _down():
        nxt = ulen_ref[u + 1]

        @pl.when(nxt > 0)
        def _():
            base = ustart_ref[u + 1]
            for_rows(padded(nxt), lambda i: row_in(base, i).start())

        wdb[...] = wd_ref[...].astype(BF16)

        @pl.when(u > 0)
        def _():
            base = ustart_ref[u - 1]
            for_rows(ulen_ref[u - 1], lambda i: row_out(base, i).wait())

        for sb in range(nsub):
            @pl.when(sb * sub < ulen)
            def _():
                r = pl.ds(sb * sub, sub)
                y = bd_ref[...] + jnp.dot(act[0, r, :], wdb[0:tf, :], preferred_element_type=F32)
                for f in range(1, nf):
                    y = y + jnp.dot(act[f, r, :], wdb[f * tf:(f + 1) * tf, :], preferred_element_type=F32)
                ybuf[r, :] = y

        own = ustart_ref[u]
        for_rows(ulen, lambda i: row_out(own, i).start())

        @pl.when(nxt == 0)
        def _():
            def single(i, c):
                row_out(own, i).wait()
                return c
            lax.fori_loop(0, ulen, single, 0)


def moe_experts(h, layer, ue, ustart, ulen, order, wg, bg, wu, bu, wd, bd, rows, sub, tf):
    n_tok, d = h.shape
    nl, ne, _, fdim = wg.shape
    nf = fdim // tf
    n_units = ue.shape[0] - 1
    last_f = nf - 1

    def f_idx(u, s, ulen_ref):
        return jnp.where(ulen_ref[u] > 0, jnp.minimum(s, last_f), last_f)

    w_in = pl.BlockSpec((None, None, d, tf),
                        lambda u, s, ue, us, ul, od: (layer, ue[u], 0, f_idx(u, s, ul)))
    b_in = pl.BlockSpec((None, None, 1, tf),
                        lambda u, s, ue, us, ul, od: (layer, ue[u], 0, f_idx(u, s, ul)))
    w_dn = pl.BlockSpec((None, None, fdim, d), lambda u, s, ue, us, ul, od: (layer, ue[u], 0, 0))
    b_dn = pl.BlockSpec((None, None, 1, d), lambda u, s, ue, us, ul, od: (layer, ue[u], 0, 0))
    any_spec = pl.BlockSpec(memory_space=pl.ANY)

    grid_spec = pltpu.PrefetchScalarGridSpec(
        num_scalar_prefetch=4,
        grid=(n_units, nf + 1),
        in_specs=[any_spec, w_in, b_in, w_in, b_in, w_dn, b_dn],
        out_specs=any_spec,
        scratch_shapes=[
            pltpu.VMEM((rows, d), F32),
            pltpu.VMEM((rows, d), BF16),
            pltpu.VMEM((d, tf), BF16),
            pltpu.VMEM((d, tf), BF16),
            pltpu.VMEM((fdim, d), BF16),
            pltpu.VMEM((nf, rows, tf), BF16),
            pltpu.VMEM((rows, d), F32),
            pltpu.SemaphoreType.DMA(()),
            pltpu.SemaphoreType.DMA(()),
        ],
    )
    return pl.pallas_call(
        functools.partial(_expert_kernel, rows=rows, sub=sub, nf=nf, tf=tf, n_tok=n_tok),
        grid_spec=grid_spec,
        out_shape=jax.ShapeDtypeStruct((n_tok * TOP_K, d), F32),
        compiler_params=_cparams(("arbitrary", "arbitrary")),
        name="moe_experts",
    )(ue, ustart, ulen, order, h, wg, bg.reshape(nl, ne, 1, fdim), wu, bu.reshape(nl, ne, 1, fdim),
      wd, bd.reshape(nl, ne, 1, d))


def routing_tables(idx_t, rows):
    k, n = idx_t.shape
    n_assign = k * n
    flat_e = idx_t.T.reshape(-1)
    order = jnp.argsort(flat_e, stable=True).astype(jnp.int32)
    counts = jnp.sum(flat_e[:, None] == jnp.arange(N_EXPERTS, dtype=jnp.int32)[None, :],
                     axis=0, dtype=jnp.int32)
    cstart = jnp.cumsum(counts) - counts
    nu = (counts + rows - 1) // rows
    ucum = jnp.cumsum(nu)
    order = jnp.concatenate([order, jnp.zeros((rows,), jnp.int32)])
    n_units = n_assign // rows + N_EXPERTS
    uid = jnp.arange(n_units + 1, dtype=jnp.int32)
    valid = uid < ucum[-1]
    e_u = jnp.minimum(jnp.searchsorted(ucum, uid, side="right"), N_EXPERTS - 1).astype(jnp.int32)
    e_last = e_u[jnp.maximum(ucum[-1] - 1, 0)]
    e_u = jnp.where(valid, e_u, e_last)
    j = uid - (ucum[e_u] - nu[e_u])
    ustart = jnp.where(valid, cstart[e_u] + j * rows, 0).astype(jnp.int32)
    ulen = jnp.where(valid, jnp.minimum(rows, counts[e_u] - j * rows), 0).astype(jnp.int32)
    return e_u, ustart, ulen, order


def _combine_ln_kernel(y0, y1, y2, y3, gt_ref, h_ref, g_ref, b_ref, o_ref):
    gt = gt_ref[...]
    f = (y0[...] * gt[:, 0:1] + y1[...] * gt[:, 1:2]
         + y2[...] * gt[:, 2:3] + y3[...] * gt[:, 3:4])
    o_ref[...] = _layer_norm(DEEPNORM_ALPHA * h_ref[...] + f, g_ref[...], b_ref[...])


def combine_residual_ln(ys, gates, h, g, b, tm=256):
    n, d = h.shape
    nb = n // tm
    vec = pl.BlockSpec((1, d), lambda i: (0, 0))

    def slot(kk):
        return pl.BlockSpec((tm, d), lambda i: (kk * nb + i, 0))

    return pl.pallas_call(
        _combine_ln_kernel,
        grid=(nb,),
        in_specs=[slot(0), slot(1), slot(2), slot(3),
                  pl.BlockSpec((tm, TOP_K), lambda i: (i, 0)),
                  pl.BlockSpec((tm, d), lambda i: (i, 0)), vec, vec],
        out_specs=pl.BlockSpec((tm, d), lambda i: (i, 0)),
        out_shape=jax.ShapeDtypeStruct((n, d), F32),
        compiler_params=_cparams(("parallel",)),
        name="moe_combine_ln",
    )(ys, ys, ys, ys, gates, h, g.reshape(1, d), b.reshape(1, d))


def _prep_mla(w_dq, q_norm, w_uq, w_dkv, kv_norm, w_ukv, w_o):
    nh = N_HEADS
    d = w_dq.shape[0]
    wuq = w_uq.reshape(Q_LORA, nh, QK_NOPE + QK_ROPE)
    pe = wuq[:, :, QK_NOPE:]
    z = jnp.zeros((Q_LORA, nh, QK_ROPE // 2), F32)
    wuq_p = jnp.concatenate([pe[:, :, 0::2], z, pe[:, :, 1::2], z], axis=-1)
    kpe = w_dkv[:, KV_LORA:]
    zk = jnp.zeros((d, QK_ROPE // 2), F32)
    wdkv = jnp.concatenate([w_dkv[:, :KV_LORA], kpe[:, 0::2], zk, kpe[:, 1::2], zk], axis=1)
    wukv = w_ukv.reshape(KV_LORA, nh, QK_NOPE + V_HEAD)
    return {
        "wdq": w_dq.astype(BF16),
        "qn": q_norm.reshape(1, Q_LORA),
        "wuq_n": wuq[:, :, :QK_NOPE].reshape(Q_LORA, nh * 128).astype(BF16),
        "wuq_p": wuq_p.reshape(Q_LORA, nh * 128).astype(BF16),
        "wdkv": wdkv.astype(BF16),
        "kvn": kv_norm.reshape(1, KV_LORA),
        "wuk": wukv[:, :, :QK_NOPE].reshape(KV_LORA, nh * 128).astype(BF16),
        "wuvt": wukv[:, :, QK_NOPE:].reshape(KV_LORA, nh * V_HEAD).T.astype(BF16),
        "wo": w_o.astype(BF16),
    }


def _rope_tables(positions):
    inv_freq = 1.0 / (ROPE_THETA ** (jnp.arange(0, QK_ROPE, 2, dtype=F32) / QK_ROPE))
    ang = positions.astype(F32).reshape(-1)[:, None] * inv_freq
    c, s = jnp.cos(ang), jnp.sin(ang)
    z = jnp.zeros_like(c)
    return jnp.concatenate([c, z, c, z], axis=1), jnp.concatenate([-s, z, s, z], axis=1)


def _dft_tables(seq, gd, tk):
    def cs(n):
        i = jnp.arange(n, dtype=jnp.int32)
        ang = ((i[:, None] * i[None, :]) % n).astype(F32) * (2.0 * math.pi / n)
        return jnp.cos(ang), jnp.sin(ang)

    cc, sc = cs(gd)
    chan = jnp.concatenate([cc, sc], axis=1).astype(BF16)

    a = 1 << ((seq.bit_length() - 1) // 2)
    col = jnp.arange(2 * seq, dtype=jnp.int32)
    within = col % (2 * tk)
    key = (col // (2 * tk)) * tk + within % tk
    quarter = (within >= tk).astype(jnp.int32) * (seq // 4)
    j1 = jnp.arange(seq // a, dtype=jnp.int32)[:, None]
    j0 = jnp.arange(a, dtype=jnp.int32)[:, None]
    ang_a = ((a * j1 * key[None, :]) % seq).astype(F32) * (2.0 * math.pi / seq)
    ang_b = ((j0 * key[None, :] + quarter[None, :]) % seq).astype(F32) * (2.0 * math.pi / seq)
    pos = (jnp.cos(ang_a)[:, None, :] * jnp.cos(ang_b)[None, :, :]
           - jnp.sin(ang_a)[:, None, :] * jnp.sin(ang_b)[None, :, :])
    return chan, pos.reshape(seq, 2 * seq).astype(BF16)


def kernel(x, positions, ln_in_g, ln_in_b, mla_w_dq, mla_q_norm, mla_w_uq, mla_w_dkv, mla_kv_norm,
           mla_w_ukv, mla_w_o, fnet_w, fnet_b, ln_mix_g, ln_mix_b, router_w, router_b, exp_w_gate,
           exp_b_gate, exp_w_up, exp_b_up, exp_w_down, exp_b_down, ln_moe_g, ln_moe_b):
    batch, seq, d = x.shape
    n = batch * seq
    t_seq = 512
    moe_rows, moe_sub, moe_tf = 512, 256, 256

    rope_c, rope_s = _rope_tables(positions)
    gd = d // FNET_GROUPS
    chan_tab, pos_tab = _dft_tables(seq, gd, t_seq)
    dft_norm = 1.0 / math.sqrt(seq * gd)
    zero_bias = jnp.zeros((d,), F32)

    h = layer_norm_rows(x.reshape(n, d), ln_in_g, ln_in_b)
    for i in range(DEPTH):
        j = i // 2
        if i % 2 == 0:
            w = _prep_mla(mla_w_dq[j], mla_q_norm[j], mla_w_uq[j], mla_w_dkv[j], mla_kv_norm[j],
                          mla_w_ukv[j], mla_w_o[j])
            q, k, vt = mla_projections(h, rope_c, rope_s, w, batch, seq, t_seq)
            o = attention(q, k, vt, tq=min(seq, 4096))
            h = proj_residual_ln(o.reshape(n, N_HEADS * V_HEAD), w["wo"], zero_bias, h,
                                 ln_mix_g[i], ln_mix_b[i])
        else:
            z = channel_dft(h, chan_tab, t_seq).reshape(batch, 2 * seq, d)
            mixed = position_dft(pos_tab, z, dft_norm, tm=1024, tk=2 * t_seq)
            h = proj_residual_ln(mixed.reshape(n, d), fnet_w[j].astype(BF16), fnet_b[j], h,
                                 ln_mix_g[i], ln_mix_b[i])
        idx_t, gate_t = router(h, router_w[i].T, router_b[i])
        ue, ustart, ulen, order = routing_tables(idx_t, moe_rows)
        ys = moe_experts(h, i, ue, ustart, ulen, order, exp_w_gate, exp_b_gate, exp_w_up,
                         exp_b_up, exp_w_down, exp_b_down, moe_rows, moe_sub, moe_tf)
        h = combine_residual_ln(ys, gate_t.T, h, ln_moe_g[i], ln_moe_b[i])
    return h.reshape(batch, seq, d)
```

```python
import functools
import math

import jax
import jax.numpy as jnp
from jax import lax
from jax.experimental import pallas as pl
from jax.experimental.pallas import tpu as pltpu

F32 = jnp.float32
BF16 = jnp.bfloat16

DEPTH = 4
N_HEADS = 16
Q_LORA = 512
KV_LORA = 512
QK_NOPE = 128
QK_ROPE = 64
V_HEAD = 128
ROPE_THETA = 10000.0
FNET_GROUPS = 4
N_EXPERTS = 32
TOP_K = 4
SWIGLU_LIMIT = 7.0
SWIGLU_ALPHA = 1.702
DEEPNORM_ALPHA = (2 * DEPTH) ** 0.25
LN_EPS = 1e-5
RMS_EPS = 1e-6

HEAD_PAD = 256
VMEM_LIMIT = 56 * 1024 * 1024
MOE_ROWS = 1280
MOE_BLOCKS = ((0, 512), (512, 256), (768, 256), (1024, 128), (1152, 128))
ROUTE_SHIFT = 13
ROUTE_TOK = 1 << ROUTE_SHIFT


def _cparams(sem):
    return pltpu.CompilerParams(dimension_semantics=sem, vmem_limit_bytes=VMEM_LIMIT)


def _layer_norm(y, g, b):
    mu = jnp.mean(y, axis=-1, keepdims=True)
    yc = y - mu
    var = jnp.mean(yc * yc, axis=-1, keepdims=True)
    return yc * lax.rsqrt(var + LN_EPS) * g + b


def _rms_norm(y, g):
    ms = jnp.mean(y * y, axis=-1, keepdims=True)
    return y * lax.rsqrt(ms + RMS_EPS) * g


def _pack_pairs(y):
    w = y.shape[1] // 2
    lo = lax.bitcast_convert_type(y[:, :w].astype(BF16).astype(F32), jnp.uint32)
    hi = lax.bitcast_convert_type(y[:, w:].astype(BF16).astype(F32), jnp.uint32)
    return lax.shift_right_logical(lo, jnp.uint32(16)) | (hi & jnp.uint32(0xFFFF0000))


def _unpack_pairs(p):
    lo = lax.bitcast_convert_type(lax.shift_left(p, jnp.uint32(16)), F32)
    hi = lax.bitcast_convert_type(p & jnp.uint32(0xFFFF0000), F32)
    return lo, hi


def _store_token_slabs(ref, row0, packed):
    rows, w = packed.shape
    nb = w // 128
    for t in range(nb):
        ref[pl.ds(row0 * nb + t, rows, stride=nb), :] = packed[:, t * 128:(t + 1) * 128]


def _load_token_slab_block(ref, row0, rows, nb, t):
    return ref[pl.ds(row0 * nb + t, rows, stride=nb), :]


def _ln_kernel(x_ref, g_ref, b_ref, o_ref):
    o_ref[...] = _layer_norm(x_ref[...], g_ref[...], b_ref[...])


def layer_norm_rows(x, g, b, tm=512):
    n, d = x.shape
    row = pl.BlockSpec((tm, d), lambda i: (i, 0))
    vec = pl.BlockSpec((1, d), lambda i: (0, 0))
    return pl.pallas_call(
        _ln_kernel,
        grid=(n // tm,),
        in_specs=[row, vec, vec],
        out_specs=row,
        out_shape=jax.ShapeDtypeStruct((n, d), F32),
        compiler_params=_cparams(("parallel",)),
        name="ln_in",
    )(x, g.reshape(1, d), b.reshape(1, d))


def _rope(x, c, s):
    return x * c + pltpu.roll(x, 64, 1) * s


def _q_proj_kernel(h_ref, c_ref, s_ref, wdq_ref, qn_ref, wn_ref, wp_ref, q_ref, *, n_heads, q_scale):
    x = h_ref[...].astype(BF16)
    qa = jnp.dot(x, wdq_ref[...], preferred_element_type=F32)
    cq = _rms_norm(qa, qn_ref[...]).astype(BF16)
    qn = jnp.dot(cq, wn_ref[...], preferred_element_type=F32) * q_scale
    qp = jnp.dot(cq, wp_ref[...], preferred_element_type=F32) * q_scale
    c = c_ref[...]
    s = s_ref[...]
    for hd in range(n_heads):
        sl = slice(hd * 128, (hd + 1) * 128)
        q_ref[hd, :, 0:128] = qn[:, sl].astype(BF16)
        q_ref[hd, :, 128:256] = _rope(qp[:, sl], c, s).astype(BF16)


def _kv_proj_kernel(h_ref, c_ref, s_ref, wdkv_ref, kvn_ref, wk_ref, wvt_ref, k_ref, vt_ref, *, n_heads, kv_lora):
    x = h_ref[...].astype(BF16)
    kva = jnp.dot(x, wdkv_ref[...], preferred_element_type=F32)
    ckv = _rms_norm(kva[:, :kv_lora], kvn_ref[...]).astype(BF16)
    kpe = _rope(kva[:, kv_lora:], c_ref[...], s_ref[...]).astype(BF16)
    kn = jnp.dot(ckv, wk_ref[...], preferred_element_type=F32)
    vt = lax.dot_general(wvt_ref[...], ckv, (((1,), (1,)), ((), ())),
                         preferred_element_type=F32)
    for hd in range(n_heads):
        sl = slice(hd * 128, (hd + 1) * 128)
        k_ref[hd, :, 0:128] = kn[:, sl].astype(BF16)
        k_ref[hd, :, 128:256] = kpe
        vt_ref[hd] = vt[sl, :].astype(BF16)


def mla_projections(h, rope_c, rope_s, w, batch, seq, tm):
    n, d = h.shape
    nh = N_HEADS
    nt = seq // tm
    q_scale = (QK_NOPE + QK_ROPE) ** -0.5 * math.log2(math.e)

    row = pl.BlockSpec((tm, d), lambda b, i: (b * nt + i, 0))
    tab = pl.BlockSpec((tm, 128), lambda b, i: (b * nt + i, 0))

    def full(a):
        return pl.BlockSpec(a.shape, lambda b, i: (0,) * a.ndim)

    head_out = pl.BlockSpec((None, nh, tm, HEAD_PAD), lambda b, i: (b, 0, i, 0))
    q = pl.pallas_call(
        functools.partial(_q_proj_kernel, n_heads=nh, q_scale=q_scale),
        grid=(batch, nt),
        in_specs=[row, tab, tab, full(w["wdq"]), full(w["qn"]), full(w["wuq_n"]), full(w["wuq_p"])],
        out_specs=head_out,
        out_shape=jax.ShapeDtypeStruct((batch, nh, seq, HEAD_PAD), BF16),
        compiler_params=_cparams(("parallel", "parallel")),
        name="mla_q_proj",
    )(h, rope_c, rope_s, w["wdq"], w["qn"], w["wuq_n"], w["wuq_p"])

    k, vt = pl.pallas_call(
        functools.partial(_kv_proj_kernel, n_heads=nh, kv_lora=KV_LORA),
        grid=(batch, nt),
        in_specs=[row, tab, tab, full(w["wdkv"]), full(w["kvn"]), full(w["wuk"]), full(w["wuvt"])],
        out_specs=[head_out,
                   pl.BlockSpec((None, nh, None, V_HEAD, tm), lambda b, i: (b, 0, i, 0, 0))],
        out_shape=[jax.ShapeDtypeStruct((batch, nh, seq, HEAD_PAD), BF16),
                   jax.ShapeDtypeStruct((batch, nh, nt, V_HEAD, tm), BF16)],
        compiler_params=_cparams(("parallel", "parallel")),
        name="mla_kv_proj",
    )(h, rope_c, rope_s, w["wdkv"], w["kvn"], w["wuk"], w["wuvt"])
    return q, k, vt


def _attn_kernel(q_ref, k_ref, vt_ref, o_ref, *, tq, tk):
    seq = q_ref.shape[0]
    nq = seq // tq
    nk = seq // tk

    def q_body(qi, carry):
        q0 = pl.multiple_of(qi * tq, tq)
        q = q_ref[pl.ds(q0, tq), :]

        def kv_body(j, st):
            m, l, acc = st
            k0 = pl.multiple_of(j * tk, tk)
            kc = k_ref[pl.ds(k0, tk), :]
            sc = lax.dot_general(kc, q, (((1,), (1,)), ((), ())),
                                 preferred_element_type=F32)
            m_new = jnp.maximum(m, jnp.max(sc, axis=0, keepdims=True))
            alpha = jnp.exp2(m - m_new)
            p = jnp.exp2(sc - m_new)
            l = alpha * l + jnp.sum(p, axis=0, keepdims=True)
            pv = jnp.dot(vt_ref[j], p.astype(BF16), preferred_element_type=F32)
            return m_new, l, alpha * acc + pv

        init = (jnp.full((1, tq), -1e30, F32), jnp.zeros((1, tq), F32), jnp.zeros((V_HEAD, tq), F32))
        _, l, acc = lax.fori_loop(0, nk, kv_body, init)
        o_ref[pl.ds(q0, tq), :] = (acc / l).T.astype(BF16)
        return carry

    lax.fori_loop(0, nq, q_body, 0)


def attention(q, k, vt, tq):
    batch, nh, seq, _ = q.shape
    nt, tk = vt.shape[2], vt.shape[4]
    return pl.pallas_call(
        functools.partial(_attn_kernel, tq=tq, tk=tk),
        grid=(batch, nh),
        in_specs=[pl.BlockSpec((None, None, seq, HEAD_PAD), lambda b, h: (b, h, 0, 0)),
                  pl.BlockSpec((None, None, seq, HEAD_PAD), lambda b, h: (b, h, 0, 0)),
                  pl.BlockSpec((None, None, nt, V_HEAD, tk), lambda b, h: (b, h, 0, 0, 0))],
        out_specs=pl.BlockSpec((None, seq, V_HEAD), lambda b, h: (b, 0, h)),
        out_shape=jax.ShapeDtypeStruct((batch, seq, nh * V_HEAD), BF16),
        compiler_params=_cparams(("parallel", "parallel")),
        name="mla_attention",
    )(q, k, vt)


def _proj_ln_kernel(a_ref, w_ref, bias_ref, h_ref, g_ref, b_ref, o_ref, hp_ref):
    m = jnp.dot(a_ref[...], w_ref[...], preferred_element_type=F32) + bias_ref[...]
    o = _layer_norm(DEEPNORM_ALPHA * h_ref[...] + m, g_ref[...], b_ref[...])
    o_ref[...] = o
    _store_token_slabs(hp_ref, 0, _pack_pairs(o))


def proj_residual_ln(a, w, bias, h, g, b, tm=512):
    n, d = h.shape
    kdim = a.shape[1]
    nb = d // 256
    vec = pl.BlockSpec((1, d), lambda i: (0, 0))
    return pl.pallas_call(
        _proj_ln_kernel,
        grid=(n // tm,),
        in_specs=[pl.BlockSpec((tm, kdim), lambda i: (i, 0)),
                  pl.BlockSpec((kdim, d), lambda i: (0, 0)),
                  vec,
                  pl.BlockSpec((tm, d), lambda i: (i, 0)),
                  vec, vec],
        out_specs=[pl.BlockSpec((tm, d), lambda i: (i, 0)),
                   pl.BlockSpec((tm * nb, 128), lambda i: (i, 0))],
        out_shape=[jax.ShapeDtypeStruct((n, d), F32),
                   jax.ShapeDtypeStruct((n * nb, 128), jnp.uint32)],
        compiler_params=_cparams(("parallel",)),
        name="proj_residual_ln",
    )(a, w, bias.reshape(1, d), h, g.reshape(1, d), b.reshape(1, d))


def _chan_dft_kernel(h_ref, t_ref, z_ref, *, groups, gd):
    t = t_ref[...]
    for g in range(groups):
        sl = slice(g * gd, (g + 1) * gd)
        xg = h_ref[:, sl].astype(BF16)
        zz = jnp.dot(xg, t, preferred_element_type=F32)
        z_ref[0, :, sl] = zz[:, :gd].astype(BF16)
        z_ref[1, :, sl] = zz[:, gd:].astype(BF16)


def channel_dft(h, tab, tm):
    n, d = h.shape
    gd = d // FNET_GROUPS
    return pl.pallas_call(
        functools.partial(_chan_dft_kernel, groups=FNET_GROUPS, gd=gd),
        grid=(n // tm,),
        in_specs=[pl.BlockSpec((tm, d), lambda i: (i, 0)),
                  pl.BlockSpec((gd, 2 * gd), lambda i: (0, 0))],
        out_specs=pl.BlockSpec((None, 2, tm, d), lambda i: (i, 0, 0, 0)),
        out_shape=jax.ShapeDtypeStruct((n // tm, 2, tm, d), BF16),
        compiler_params=_cparams(("parallel",)),
        name="fnet_channel_dft",
    )(h, tab)


def _pos_dft_kernel(t_ref, z_ref, o_ref, acc_ref, *, norm):
    k = pl.program_id(2)

    @pl.when(k == 0)
    def _():
        acc_ref[...] = jnp.zeros_like(acc_ref)

    acc_ref[...] += jnp.dot(t_ref[...], z_ref[...], preferred_element_type=F32)

    @pl.when(k == pl.num_programs(2) - 1)
    def _():
        o_ref[...] = (acc_ref[...] * norm).astype(BF16)


def position_dft(tab, z, norm, tm, tk):
    batch, k2, d = z.shape
    seq = tab.shape[0]
    return pl.pallas_call(
        functools.partial(_pos_dft_kernel, norm=norm),
        grid=(batch, seq // tm, k2 // tk),
        in_specs=[pl.BlockSpec((tm, tk), lambda b, i, k: (i, k)),
                  pl.BlockSpec((None, tk, d), lambda b, i, k: (b, k, 0))],
        out_specs=pl.BlockSpec((None, tm, d), lambda b, i, k: (b, i, 0)),
        out_shape=jax.ShapeDtypeStruct((batch, seq, d), BF16),
        scratch_shapes=[pltpu.VMEM((tm, d), F32)],
        compiler_params=_cparams(("parallel", "parallel", "arbitrary")),
        name="fnet_position_dft",
    )(tab, z)


def _split_bf16(x):
    hi = x.astype(BF16)
    lo = (x - hi.astype(F32)).astype(BF16)
    return hi, lo


def _router_kernel(h_ref, wt_ref, b_ref, idx_ref, gate_ref, *, top_k):
    nt = (((1,), (1,)), ((), ()))
    x_hi, x_lo = _split_bf16(h_ref[...])
    w_hi, w_lo = _split_bf16(wt_ref[...])
    logits = (lax.dot_general(w_hi, x_hi, nt, preferred_element_type=F32)
              + lax.dot_general(w_hi, x_lo, nt, preferred_element_type=F32)
              + lax.dot_general(w_lo, x_hi, nt, preferred_element_type=F32)
              + b_ref[...])
    ne = logits.shape[0]
    eid = lax.broadcasted_iota(jnp.int32, logits.shape, 0)
    vals, idxs = [], []
    cur = logits
    for _ in range(top_k):
        mx = jnp.max(cur, axis=0, keepdims=True)
        ix = jnp.min(jnp.where(cur == mx, eid, ne), axis=0, keepdims=True)
        vals.append(mx)
        idxs.append(ix)
        cur = jnp.where(eid == ix, -jnp.inf, cur)
    ex = [jnp.exp(v - vals[0]) for v in vals]
    tot = ex[0]
    for e in ex[1:]:
        tot = tot + e
    for kk in range(top_k):
        idx_ref[kk:kk + 1, :] = idxs[kk]
        gate_ref[kk:kk + 1, :] = ex[kk] / tot


def router(h, w_t, bias, tm=512):
    n, d = h.shape
    ne = w_t.shape[0]
    return pl.pallas_call(
        functools.partial(_router_kernel, top_k=TOP_K),
        grid=(n // tm,),
        in_specs=[pl.BlockSpec((tm, d), lambda i: (i, 0)),
                  pl.BlockSpec((ne, d), lambda i: (0, 0)),
                  pl.BlockSpec((ne, 1), lambda i: (0, 0))],
        out_specs=[pl.BlockSpec((TOP_K, tm), lambda i: (0, i)),
                   pl.BlockSpec((TOP_K, tm), lambda i: (0, i))],
        out_shape=[jax.ShapeDtypeStruct((TOP_K, n), jnp.int32),
                   jax.ShapeDtypeStruct((TOP_K, n), F32)],
        compiler_params=_cparams(("parallel",)),
        name="moe_router",
    )(h, w_t, bias.reshape(ne, 1))


def _swiglu(hg, hu):
    g = jnp.minimum(hg, SWIGLU_LIMIT)
    l = jnp.clip(hu, -SWIGLU_LIMIT, SWIGLU_LIMIT)
    return g * (1.0 / (1.0 + jnp.exp(-SWIGLU_ALPHA * g))) * (l + 1.0)


def _expert_kernel(ue_ref, ustart_ref, ulen_ref, route_ref,
                   hp_hbm, wg_ref, bg_ref, wu_ref, bu_ref, wd_ref, bd_ref,
                   ys_hbm,
                   stage, xb, wgb, wub, wdb, act, ybuf, sem_g, sem_s,
                   *, rows, blocks, nf, tf, n_tok, nb):
    u = pl.program_id(0)
    s = pl.program_id(1)
    ulen = ulen_ref[u]
    chunk = rows // nf
    half = xb.shape[1] // 2
    n_assign = TOP_K * n_tok

    def row_in(base, i):
        tok = route_ref[base + i] & (ROUTE_TOK - 1)
        return pltpu.make_async_copy(hp_hbm.at[pl.ds(pl.multiple_of(tok * nb, nb), nb), :],
                                     stage.at[pl.ds(pl.multiple_of(i * nb, nb), nb), :], sem_g)

    def row_out(base, i):
        dest = lax.shift_right_logical(route_ref[base + i], ROUTE_SHIFT)
        return pltpu.make_async_copy(ybuf.at[pl.ds(pl.multiple_of(i * nb, nb), nb), :],
                                     ys_hbm.at[pl.ds(pl.multiple_of(dest * nb, nb), nb), :], sem_s)

    def loop_rows(fn):
        def single(i, c):
            fn(i)
            return c
        lax.fori_loop(0, rows, single, 0)

    def wait_rows(make):
        for i in range(rows):
            make(i).wait()

    def prev_base():
        return jnp.where(u > 0, ustart_ref[jnp.maximum(u - 1, 0)], n_assign)

    @pl.when(jnp.logical_and(u == 0, s == 0))
    def _first_step():
        ybuf[...] = jnp.zeros_like(ybuf)
        base = ustart_ref[0]
        loop_rows(lambda i: row_in(base, i).start())

    @pl.when(jnp.logical_and(s == 0, ulen > 0))
    def _await_rows():
        base = ustart_ref[u]
        wait_rows(lambda i: row_in(base, i))
        for start, size in blocks:
            @pl.when(start < ulen)
            def _():
                r = pl.ds(start, size)
                for t in range(nb):
                    lo, hi = _unpack_pairs(_load_token_slab_block(stage, start, size, nb, t))
                    xb[r, t * 128:(t + 1) * 128] = lo.astype(BF16)
                    xb[r, half + t * 128:half + (t + 1) * 128] = hi.astype(BF16)

    @pl.when(jnp.logical_and(s < nf, ulen > 0))
    def _gate_up():
        base_next = ustart_ref[u + 1]
        base_prev = prev_base()
        row0 = s * chunk
        for c in range(chunk):
            row_in(base_next, row0 + c).start()
            row_out(base_prev, row0 + c).start()

        wgb[...] = wg_ref[...].astype(BF16)
        wub[...] = wu_ref[...].astype(BF16)
        for start, size in blocks:
            @pl.when(start < ulen)
            def _():
                r = pl.ds(start, size)
                x = xb[r, :]
                hg = jnp.dot(x, wgb[...], preferred_element_type=F32) + bg_ref[...]
                hu = jnp.dot(x, wub[...], preferred_element_type=F32) + bu_ref[...]
                act[s, r, :] = _swiglu(hg, hu).astype(BF16)

    @pl.when(jnp.logical_and(s == nf, ulen > 0))
    def _down():
        wdb[...] = wd_ref[...].astype(BF16)
        base_prev = prev_base()
        wait_rows(lambda i: row_out(base_prev, i))
        for start, size in blocks:
            @pl.when(start < ulen)
            def _():
                r = pl.ds(start, size)
                y = bd_ref[...] + jnp.dot(act[0, r, :], wdb[0:tf, :], preferred_element_type=F32)
                for f in range(1, nf):
                    y = y + jnp.dot(act[f, r, :], wdb[f * tf:(f + 1) * tf, :], preferred_element_type=F32)
                _store_token_slabs(ybuf, start, _pack_pairs(y))

        @pl.when(ulen_ref[u + 1] == 0)
        def _():
            own = ustart_ref[u]
            base_next = ustart_ref[u + 1]
            loop_rows(lambda i: row_out(own, i).start())
            loop_rows(lambda i: row_out(own, i).wait())
            loop_rows(lambda i: row_in(base_next, i).wait())


def moe_experts(hp, layer, ue, ustart, ulen, route, wg, bg, wu, bu, wd, bd, rows, tf):
    nl, ne, d, fdim = wg.shape
    nb = d // 256
    n_tok = hp.shape[0] // nb
    nf = fdim // tf
    n_units = ue.shape[0] - 1
    last_f = nf - 1
    blocks = tuple((st, min(sz, rows - st)) for st, sz in MOE_BLOCKS if st < rows)
    assert rows % nf == 0 and sum(sz for _, sz in blocks) == rows and n_tok <= ROUTE_TOK

    def f_idx(u, s, ulen_ref):
        return jnp.where(ulen_ref[u] > 0, jnp.minimum(s, last_f), last_f)

    w_in = pl.BlockSpec((None, None, d, tf),
                        lambda u, s, ue, us, ul, od: (layer, ue[u], 0, f_idx(u, s, ul)))
    b_in = pl.BlockSpec((None, None, 1, tf),
                        lambda u, s, ue, us, ul, od: (layer, ue[u], 0, f_idx(u, s, ul)))
    w_dn = pl.BlockSpec((None, None, fdim, d), lambda u, s, ue, us, ul, od: (layer, ue[u], 0, 0))
    b_dn = pl.BlockSpec((None, None, 1, d), lambda u, s, ue, us, ul, od: (layer, ue[u], 0, 0))
    any_spec = pl.BlockSpec(memory_space=pl.ANY)

    grid_spec = pltpu.PrefetchScalarGridSpec(
        num_scalar_prefetch=4,
        grid=(n_units, nf + 1),
        in_specs=[any_spec, w_in, b_in, w_in, b_in, w_dn, b_dn],
        out_specs=any_spec,
        scratch_shapes=[
            pltpu.VMEM((rows * nb, 128), jnp.uint32),
            pltpu.VMEM((rows, d), BF16),
            pltpu.VMEM((d, tf), BF16),
            pltpu.VMEM((d, tf), BF16),
            pltpu.VMEM((fdim, d), BF16),
            pltpu.VMEM((nf, rows, tf), BF16),
            pltpu.VMEM((rows * nb, 128), jnp.uint32),
            pltpu.SemaphoreType.DMA(()),
            pltpu.SemaphoreType.DMA(()),
        ],
    )
    return pl.pallas_call(
        functools.partial(_expert_kernel, rows=rows, blocks=blocks, nf=nf, tf=tf, n_tok=n_tok, nb=nb),
        grid_spec=grid_spec,
        out_shape=jax.ShapeDtypeStruct(((n_tok * TOP_K + rows) * nb, 128), jnp.uint32),
        compiler_params=_cparams(("arbitrary", "arbitrary")),
        name="moe_experts",
    )(ue, ustart, ulen, route, hp, wg, bg.reshape(nl, ne, 1, fdim), wu, bu.reshape(nl, ne, 1, fdim),
      wd, bd.reshape(nl, ne, 1, d))


def routing_tables(idx_t, rows):
    k, n = idx_t.shape
    n_assign = k * n
    flat_e = idx_t.T.reshape(-1)
    order = jnp.argsort(flat_e, stable=True).astype(jnp.int32)
    dest = (order % k) * n + order // k
    spare = n_assign + jnp.arange(rows, dtype=jnp.int32)
    route = jnp.concatenate([dest * ROUTE_TOK + order // k, spare * ROUTE_TOK])
    counts = jnp.sum(flat_e[:, None] == jnp.arange(N_EXPERTS, dtype=jnp.int32)[None, :],
                     axis=0, dtype=jnp.int32)
    cstart = jnp.cumsum(counts) - counts
    nu = (counts + rows - 1) // rows
    ucum = jnp.cumsum(nu)
    n_units = n_assign // rows + N_EXPERTS
    uid = jnp.arange(n_units + 1, dtype=jnp.int32)
    valid = uid < ucum[-1]
    e_u = jnp.minimum(jnp.searchsorted(ucum, uid, side="right"), N_EXPERTS - 1).astype(jnp.int32)
    e_last = e_u[jnp.maximum(ucum[-1] - 1, 0)]
    e_u = jnp.where(valid, e_u, e_last)
    j = uid - (ucum[e_u] - nu[e_u])
    per = (counts + jnp.maximum(nu, 1) - 1) // jnp.maximum(nu, 1)
    per = ((per + 7) // 8) * 8
    ustart = jnp.where(valid, cstart[e_u] + j * per[e_u], 0).astype(jnp.int32)
    ulen = jnp.where(valid, jnp.clip(counts[e_u] - j * per[e_u], 0, per[e_u]), 0).astype(jnp.int32)
    return e_u, ustart, ulen, route


def _combine_ln_kernel(y0, y1, y2, y3, gt_ref, h_ref, g_ref, b_ref, o_ref, hp_ref, f_ref, *, nb):
    tm, d = h_ref.shape
    half = d // 2
    gt = gt_ref[...]
    slots = (y0, y1, y2, y3)
    for t in range(nb):
        lo_sum = hi_sum = None
        for kk, y_ref in enumerate(slots):
            lo, hi = _unpack_pairs(_load_token_slab_block(y_ref, 0, tm, nb, t))
            gk = gt[:, kk:kk + 1]
            lo_sum = lo * gk if lo_sum is None else lo_sum + lo * gk
            hi_sum = hi * gk if hi_sum is None else hi_sum + hi * gk
        f_ref[:, t * 128:(t + 1) * 128] = lo_sum
        f_ref[:, half + t * 128:half + (t + 1) * 128] = hi_sum
    o = _layer_norm(DEEPNORM_ALPHA * h_ref[...] + f_ref[...], g_ref[...], b_ref[...])
    o_ref[...] = o
    _store_token_slabs(hp_ref, 0, _pack_pairs(o))


def combine_residual_ln(ys, gates, h, g, b, tm=256):
    n, d = h.shape
    nb = d // 256
    nblk = n // tm
    vec = pl.BlockSpec((1, d), lambda i: (0, 0))

    def slot(kk):
        return pl.BlockSpec((tm * nb, 128), lambda i: (kk * nblk + i, 0))

    return pl.pallas_call(
        functools.partial(_combine_ln_kernel, nb=nb),
        grid=(nblk,),
        in_specs=[slot(0), slot(1), slot(2), slot(3),
                  pl.BlockSpec((tm, TOP_K), lambda i: (i, 0)),
                  pl.BlockSpec((tm, d), lambda i: (i, 0)), vec, vec],
        out_specs=[pl.BlockSpec((tm, d), lambda i: (i, 0)),
                   pl.BlockSpec((tm * nb, 128), lambda i: (i, 0))],
        out_shape=[jax.ShapeDtypeStruct((n, d), F32),
                   jax.ShapeDtypeStruct((n * nb, 128), jnp.uint32)],
        scratch_shapes=[pltpu.VMEM((tm, d), F32)],
        compiler_params=_cparams(("parallel",)),
        name="moe_combine_ln",
    )(ys, ys, ys, ys, gates, h, g.reshape(1, d), b.reshape(1, d))


def _prep_mla(w_dq, q_norm, w_uq, w_dkv, kv_norm, w_ukv, w_o):
    nh = N_HEADS
    d = w_dq.shape[0]
    wuq = w_uq.reshape(Q_LORA, nh, QK_NOPE + QK_ROPE)
    pe = wuq[:, :, QK_NOPE:]
    z = jnp.zeros((Q_LORA, nh, QK_ROPE // 2), F32)
    wuq_p = jnp.concatenate([pe[:, :, 0::2], z, pe[:, :, 1::2], z], axis=-1)
    kpe = w_dkv[:, KV_LORA:]
    zk = jnp.zeros((d, QK_ROPE // 2), F32)
    wdkv = jnp.concatenate([w_dkv[:, :KV_LORA], kpe[:, 0::2], zk, kpe[:, 1::2], zk], axis=1)
    wukv = w_ukv.reshape(KV_LORA, nh, QK_NOPE + V_HEAD)
    return {
        "wdq": w_dq.astype(BF16),
        "qn": q_norm.reshape(1, Q_LORA),
        "wuq_n": wuq[:, :, :QK_NOPE].reshape(Q_LORA, nh * 128).astype(BF16),
        "wuq_p": wuq_p.reshape(Q_LORA, nh * 128).astype(BF16),
        "wdkv": wdkv.astype(BF16),
        "kvn": kv_norm.reshape(1, KV_LORA),
        "wuk": wukv[:, :, :QK_NOPE].reshape(KV_LORA, nh * 128).astype(BF16),
        "wuvt": wukv[:, :, QK_NOPE:].reshape(KV_LORA, nh * V_HEAD).T.astype(BF16),
        "wo": w_o.astype(BF16),
    }


def _rope_tables(positions):
    inv_freq = 1.0 / (ROPE_THETA ** (jnp.arange(0, QK_ROPE, 2, dtype=F32) / QK_ROPE))
    ang = positions.astype(F32).reshape(-1)[:, None] * inv_freq
    c, s = jnp.cos(ang), jnp.sin(ang)
    z = jnp.zeros_like(c)
    return jnp.concatenate([c, z, c, z], axis=1), jnp.concatenate([-s, z, s, z], axis=1)


def _dft_tables(seq, gd, tk):
    def cs(n):
        i = jnp.arange(n, dtype=jnp.int32)
        ang = ((i[:, None] * i[None, :]) % n).astype(F32) * (2.0 * math.pi / n)
        return jnp.cos(ang), jnp.sin(ang)

    cc, sc = cs(gd)
    chan = jnp.concatenate([cc, sc], axis=1).astype(BF16)

    a = 1 << ((seq.bit_length() - 1) // 2)
    col = jnp.arange(2 * seq, dtype=jnp.int32)
    within = col % (2 * tk)
    key = (col // (2 * tk)) * tk + within % tk
    quarter = (within >= tk).astype(jnp.int32) * (seq // 4)
    j1 = jnp.arange(seq // a, dtype=jnp.int32)[:, None]
    j0 = jnp.arange(a, dtype=jnp.int32)[:, None]
    ang_a = ((a * j1 * key[None, :]) % seq).astype(F32) * (2.0 * math.pi / seq)
    ang_b = ((j0 * key[None, :] + quarter[None, :]) % seq).astype(F32) * (2.0 * math.pi / seq)
    pos = (jnp.cos(ang_a)[:, None, :] * jnp.cos(ang_b)[None, :, :]
           - jnp.sin(ang_a)[:, None, :] * jnp.sin(ang_b)[None, :, :])
    return chan, pos.reshape(seq, 2 * seq).astype(BF16)


def kernel(x, positions, ln_in_g, ln_in_b, mla_w_dq, mla_q_norm, mla_w_uq, mla_w_dkv, mla_kv_norm,
           mla_w_ukv, mla_w_o, fnet_w, fnet_b, ln_mix_g, ln_mix_b, router_w, router_b, exp_w_gate,
           exp_b_gate, exp_w_up, exp_b_up, exp_w_down, exp_b_down, ln_moe_g, ln_moe_b):
    batch, seq, d = x.shape
    n = batch * seq
    t_seq = 512
    moe_tf = 256

    rope_c, rope_s = _rope_tables(positions)
    gd = d // FNET_GROUPS
    chan_tab, pos_tab = _dft_tables(seq, gd, t_seq)
    dft_norm = 1.0 / math.sqrt(seq * gd)
    zero_bias = jnp.zeros((d,), F32)

    h = layer_norm_rows(x.reshape(n, d), ln_in_g, ln_in_b)
    for i in range(DEPTH):
        j = i // 2
        if i % 2 == 0:
            w = _prep_mla(mla_w_dq[j], mla_q_norm[j], mla_w_uq[j], mla_w_dkv[j], mla_kv_norm[j],
                          mla_w_ukv[j], mla_w_o[j])
            q, k, vt = mla_projections(h, rope_c, rope_s, w, batch, seq, t_seq)
            o = attention(q, k, vt, tq=min(seq, 4096))
            h, hp = proj_residual_ln(o.reshape(n, N_HEADS * V_HEAD), w["wo"], zero_bias, h,
                                     ln_mix_g[i], ln_mix_b[i])
        else:
            z = channel_dft(h, chan_tab, t_seq).reshape(batch, 2 * seq, d)
            mixed = position_dft(pos_tab, z, dft_norm, tm=1024, tk=2 * t_seq)
            h, hp = proj_residual_ln(mixed.reshape(n, d), fnet_w[j].astype(BF16), fnet_b[j], h,
                                     ln_mix_g[i], ln_mix_b[i])
        idx_t, gate_t = router(h, router_w[i].T, router_b[i])
        ue, ustart, ulen, route = routing_tables(idx_t, MOE_ROWS)
        ys = moe_experts(hp, i, ue, ustart, ulen, route, exp_w_gate, exp_b_gate, exp_w_up,
                         exp_b_up, exp_w_down, exp_b_down, MOE_ROWS, moe_tf)
        h, _ = combine_residual_ln(ys, gate_t.T, h, ln_moe_g[i], ln_moe_b[i])
    return h.reshape(batch, seq, d)
```

```python
import functools
import math

import jax
import jax.numpy as jnp
from jax import lax
from jax.experimental import pallas as pl
from jax.experimental.pallas import tpu as pltpu

F32 = jnp.float32
BF16 = jnp.bfloat16

DEPTH = 4
N_HEADS = 16
Q_LORA = 512
KV_LORA = 512
QK_NOPE = 128
QK_ROPE = 64
V_HEAD = 128
ROPE_THETA = 10000.0
FNET_GROUPS = 4
N_EXPERTS = 32
TOP_K = 4
SWIGLU_LIMIT = 7.0
SWIGLU_ALPHA = 1.702
DEEPNORM_ALPHA = (2 * DEPTH) ** 0.25
LN_EPS = 1e-5
RMS_EPS = 1e-6

HEAD_PAD = 256
VMEM_LIMIT = 56 * 1024 * 1024
MOE_ROWS = 1280
MOE_BLOCKS = ((0, 768), (768, 256), (1024, 128), (1152, 128))
MOE_GROUP = 64
MOE_SPARE = 256
ROUTE_SHIFT = 13
ROUTE_TOK = 1 << ROUTE_SHIFT


def _cparams(sem):
    return pltpu.CompilerParams(dimension_semantics=sem, vmem_limit_bytes=VMEM_LIMIT)


def _layer_norm(y, g, b):
    mu = jnp.mean(y, axis=-1, keepdims=True)
    yc = y - mu
    var = jnp.mean(yc * yc, axis=-1, keepdims=True)
    return yc * lax.rsqrt(var + LN_EPS) * g + b


def _rms_norm(y, g):
    ms = jnp.mean(y * y, axis=-1, keepdims=True)
    return y * lax.rsqrt(ms + RMS_EPS) * g


def _pack_pairs(y):
    w = y.shape[1] // 2
    lo = lax.bitcast_convert_type(y[:, :w].astype(BF16).astype(F32), jnp.uint32)
    hi = lax.bitcast_convert_type(y[:, w:].astype(BF16).astype(F32), jnp.uint32)
    return lax.shift_right_logical(lo, jnp.uint32(16)) | (hi & jnp.uint32(0xFFFF0000))


def _unpack_pairs(p):
    lo = lax.bitcast_convert_type(lax.shift_left(p, jnp.uint32(16)), F32)
    hi = lax.bitcast_convert_type(p & jnp.uint32(0xFFFF0000), F32)
    return lo, hi


def _store_token_slabs(ref, row0, packed):
    rows, w = packed.shape
    nb = w // 128
    for t in range(nb):
        ref[pl.ds(row0 * nb + t, rows, stride=nb), :] = packed[:, t * 128:(t + 1) * 128]


def _load_token_slab_block(ref, row0, rows, nb, t):
    return ref[pl.ds(row0 * nb + t, rows, stride=nb), :]


def _ln_kernel(x_ref, g_ref, b_ref, o_ref):
    o_ref[...] = _layer_norm(x_ref[...], g_ref[...], b_ref[...])


def layer_norm_rows(x, g, b, tm=512):
    n, d = x.shape
    row = pl.BlockSpec((tm, d), lambda i: (i, 0))
    vec = pl.BlockSpec((1, d), lambda i: (0, 0))
    return pl.pallas_call(
        _ln_kernel,
        grid=(n // tm,),
        in_specs=[row, vec, vec],
        out_specs=row,
        out_shape=jax.ShapeDtypeStruct((n, d), F32),
        compiler_params=_cparams(("parallel",)),
        name="ln_in",
    )(x, g.reshape(1, d), b.reshape(1, d))


def _rope(x, c, s):
    return x * c + pltpu.roll(x, 64, 1) * s


def _q_proj_kernel(h_ref, c_ref, s_ref, wdq_ref, qn_ref, wn_ref, wp_ref, q_ref, *, n_heads, q_scale):
    x = h_ref[...].astype(BF16)
    qa = jnp.dot(x, wdq_ref[...], preferred_element_type=F32)
    cq = _rms_norm(qa, qn_ref[...]).astype(BF16)
    qn = jnp.dot(cq, wn_ref[...], preferred_element_type=F32) * q_scale
    qp = jnp.dot(cq, wp_ref[...], preferred_element_type=F32) * q_scale
    c = c_ref[...]
    s = s_ref[...]
    for hd in range(n_heads):
        sl = slice(hd * 128, (hd + 1) * 128)
        q_ref[hd, :, 0:128] = qn[:, sl].astype(BF16)
        q_ref[hd, :, 128:256] = _rope(qp[:, sl], c, s).astype(BF16)


def _kv_proj_kernel(h_ref, c_ref, s_ref, wdkv_ref, kvn_ref, wk_ref, wvt_ref, k_ref, vt_ref, *, n_heads, kv_lora):
    x = h_ref[...].astype(BF16)
    kva = jnp.dot(x, wdkv_ref[...], preferred_element_type=F32)
    ckv = _rms_norm(kva[:, :kv_lora], kvn_ref[...]).astype(BF16)
    kpe = _rope(kva[:, kv_lora:], c_ref[...], s_ref[...]).astype(BF16)
    kn = jnp.dot(ckv, wk_ref[...], preferred_element_type=F32)
    vt = lax.dot_general(wvt_ref[...], ckv, (((1,), (1,)), ((), ())),
                         preferred_element_type=F32)
    for hd in range(n_heads):
        sl = slice(hd * 128, (hd + 1) * 128)
        k_ref[hd, :, 0:128] = kn[:, sl].astype(BF16)
        k_ref[hd, :, 128:256] = kpe
        vt_ref[hd] = vt[sl, :].astype(BF16)


def mla_projections(h, rope_c, rope_s, w, batch, seq, tm):
    n, d = h.shape
    nh = N_HEADS
    nt = seq // tm
    q_scale = (QK_NOPE + QK_ROPE) ** -0.5 * math.log2(math.e)

    row = pl.BlockSpec((tm, d), lambda b, i: (b * nt + i, 0))
    tab = pl.BlockSpec((tm, 128), lambda b, i: (b * nt + i, 0))

    def full(a):
        return pl.BlockSpec(a.shape, lambda b, i: (0,) * a.ndim)

    head_out = pl.BlockSpec((None, nh, tm, HEAD_PAD), lambda b, i: (b, 0, i, 0))
    q = pl.pallas_call(
        functools.partial(_q_proj_kernel, n_heads=nh, q_scale=q_scale),
        grid=(batch, nt),
        in_specs=[row, tab, tab, full(w["wdq"]), full(w["qn"]), full(w["wuq_n"]), full(w["wuq_p"])],
        out_specs=head_out,
        out_shape=jax.ShapeDtypeStruct((batch, nh, seq, HEAD_PAD), BF16),
        compiler_params=_cparams(("parallel", "parallel")),
        name="mla_q_proj",
    )(h, rope_c, rope_s, w["wdq"], w["qn"], w["wuq_n"], w["wuq_p"])

    k, vt = pl.pallas_call(
        functools.partial(_kv_proj_kernel, n_heads=nh, kv_lora=KV_LORA),
        grid=(batch, nt),
        in_specs=[row, tab, tab, full(w["wdkv"]), full(w["kvn"]), full(w["wuk"]), full(w["wuvt"])],
        out_specs=[head_out,
                   pl.BlockSpec((None, nh, None, V_HEAD, tm), lambda b, i: (b, 0, i, 0, 0))],
        out_shape=[jax.ShapeDtypeStruct((batch, nh, seq, HEAD_PAD), BF16),
                   jax.ShapeDtypeStruct((batch, nh, nt, V_HEAD, tm), BF16)],
        compiler_params=_cparams(("parallel", "parallel")),
        name="mla_kv_proj",
    )(h, rope_c, rope_s, w["wdkv"], w["kvn"], w["wuk"], w["wuvt"])
    return q, k, vt


def _attn_kernel(q_ref, k_ref, vt_ref, o_ref, *, tq, tk):
    seq = q_ref.shape[0]
    nq = seq // tq
    nk = seq // tk

    def q_body(qi, carry):
        q0 = pl.multiple_of(qi * tq, tq)
        q = q_ref[pl.ds(q0, tq), :]

        def kv_body(j, st):
            m, l, acc = st
            k0 = pl.multiple_of(j * tk, tk)
            kc = k_ref[pl.ds(k0, tk), :]
            sc = lax.dot_general(kc, q, (((1,), (1,)), ((), ())),
                                 preferred_element_type=F32)
            m_new = jnp.maximum(m, jnp.max(sc, axis=0, keepdims=True))
            alpha = jnp.exp2(m - m_new)
            p = jnp.exp2(sc - m_new)
            l = alpha * l + jnp.sum(p, axis=0, keepdims=True)
            pv = jnp.dot(vt_ref[j], p.astype(BF16), preferred_element_type=F32)
            return m_new, l, alpha * acc + pv

        init = (jnp.full((1, tq), -1e30, F32), jnp.zeros((1, tq), F32), jnp.zeros((V_HEAD, tq), F32))
        _, l, acc = lax.fori_loop(0, nk, kv_body, init)
        o_ref[pl.ds(q0, tq), :] = (acc / l).T.astype(BF16)
        return carry

    lax.fori_loop(0, nq, q_body, 0)


def attention(q, k, vt, tq):
    batch, nh, seq, _ = q.shape
    nt, tk = vt.shape[2], vt.shape[4]
    return pl.pallas_call(
        functools.partial(_attn_kernel, tq=tq, tk=tk),
        grid=(batch, nh),
        in_specs=[pl.BlockSpec((None, None, seq, HEAD_PAD), lambda b, h: (b, h, 0, 0)),
                  pl.BlockSpec((None, None, seq, HEAD_PAD), lambda b, h: (b, h, 0, 0)),
                  pl.BlockSpec((None, None, nt, V_HEAD, tk), lambda b, h: (b, h, 0, 0, 0))],
        out_specs=pl.BlockSpec((None, seq, V_HEAD), lambda b, h: (b, 0, h)),
        out_shape=jax.ShapeDtypeStruct((batch, seq, nh * V_HEAD), BF16),
        compiler_params=_cparams(("parallel", "parallel")),
        name="mla_attention",
    )(q, k, vt)


def _split_bf16(x):
    hi = x.astype(BF16)
    lo = (x - hi.astype(F32)).astype(BF16)
    return hi, lo


def _route(x, w_t, bias, top_k):
    nt = (((1,), (1,)), ((), ()))
    x_hi, x_lo = _split_bf16(x)
    w_hi, w_lo = _split_bf16(w_t)
    logits = (lax.dot_general(w_hi, x_hi, nt, preferred_element_type=F32)
              + lax.dot_general(w_hi, x_lo, nt, preferred_element_type=F32)
              + lax.dot_general(w_lo, x_hi, nt, preferred_element_type=F32)
              + bias)
    ne = logits.shape[0]
    eid = lax.broadcasted_iota(jnp.int32, logits.shape, 0)
    vals, idxs = [], []
    cur = logits
    for _ in range(top_k):
        mx = jnp.max(cur, axis=0, keepdims=True)
        ix = jnp.min(jnp.where(cur == mx, eid, ne), axis=0, keepdims=True)
        vals.append(mx)
        idxs.append(ix)
        cur = jnp.where(eid == ix, -jnp.inf, cur)
    ex = [jnp.exp(v - vals[0]) for v in vals]
    tot = ex[0]
    for e in ex[1:]:
        tot = tot + e
    return idxs, [e / tot for e in ex]


def _proj_ln_kernel(a_ref, w_ref, bias_ref, h_ref, g_ref, b_ref, wr_ref, br_ref,
                    o_ref, hp_ref, idx_ref, gate_ref):
    m = jnp.dot(a_ref[...], w_ref[...], preferred_element_type=F32) + bias_ref[...]
    o = _layer_norm(DEEPNORM_ALPHA * h_ref[...] + m, g_ref[...], b_ref[...])
    o_ref[...] = o
    _store_token_slabs(hp_ref, 0, _pack_pairs(o))
    idxs, gates = _route(o, wr_ref[...], br_ref[...], TOP_K)
    for kk in range(TOP_K):
        idx_ref[kk:kk + 1, :] = idxs[kk]
        gate_ref[kk:kk + 1, :] = gates[kk]


def proj_residual_ln(a, w, bias, h, g, b, wr_t, br, tm=512):
    n, d = h.shape
    kdim = a.shape[1]
    ne = wr_t.shape[0]
    nb = d // 256
    vec = pl.BlockSpec((1, d), lambda i: (0, 0))
    return pl.pallas_call(
        _proj_ln_kernel,
        grid=(n // tm,),
        in_specs=[pl.BlockSpec((tm, kdim), lambda i: (i, 0)),
                  pl.BlockSpec((kdim, d), lambda i: (0, 0)),
                  vec,
                  pl.BlockSpec((tm, d), lambda i: (i, 0)),
                  vec, vec,
                  pl.BlockSpec((ne, d), lambda i: (0, 0)),
                  pl.BlockSpec((ne, 1), lambda i: (0, 0))],
        out_specs=[pl.BlockSpec((tm, d), lambda i: (i, 0)),
                   pl.BlockSpec((tm * nb, 128), lambda i: (i, 0)),
                   pl.BlockSpec((TOP_K, tm), lambda i: (0, i)),
                   pl.BlockSpec((TOP_K, tm), lambda i: (0, i))],
        out_shape=[jax.ShapeDtypeStruct((n, d), F32),
                   jax.ShapeDtypeStruct((n * nb, 128), jnp.uint32),
                   jax.ShapeDtypeStruct((TOP_K, n), jnp.int32),
                   jax.ShapeDtypeStruct((TOP_K, n), F32)],
        compiler_params=_cparams(("parallel",)),
        name="proj_residual_ln",
    )(a, w, bias.reshape(1, d), h, g.reshape(1, d), b.reshape(1, d), wr_t, br.reshape(ne, 1))


def _chan_dft_kernel(h_ref, t_ref, z_ref, *, groups, gd):
    t = t_ref[...]
    for g in range(groups):
        sl = slice(g * gd, (g + 1) * gd)
        xg = h_ref[:, sl].astype(BF16)
        zz = jnp.dot(xg, t, preferred_element_type=F32)
        z_ref[0, :, sl] = zz[:, :gd].astype(BF16)
        z_ref[1, :, sl] = zz[:, gd:].astype(BF16)


def channel_dft(h, tab, tm):
    n, d = h.shape
    gd = d // FNET_GROUPS
    return pl.pallas_call(
        functools.partial(_chan_dft_kernel, groups=FNET_GROUPS, gd=gd),
        grid=(n // tm,),
        in_specs=[pl.BlockSpec((tm, d), lambda i: (i, 0)),
                  pl.BlockSpec((gd, 2 * gd), lambda i: (0, 0))],
        out_specs=pl.BlockSpec((None, 2, tm, d), lambda i: (i, 0, 0, 0)),
        out_shape=jax.ShapeDtypeStruct((n // tm, 2, tm, d), BF16),
        compiler_params=_cparams(("parallel",)),
        name="fnet_channel_dft",
    )(h, tab)


def _pos_dft_kernel(t_ref, z_ref, o_ref, acc_ref, *, norm):
    k = pl.program_id(2)

    @pl.when(k == 0)
    def _():
        acc_ref[...] = jnp.zeros_like(acc_ref)

    acc_ref[...] += jnp.dot(t_ref[...], z_ref[...], preferred_element_type=F32)

    @pl.when(k == pl.num_programs(2) - 1)
    def _():
        o_ref[...] = (acc_ref[...] * norm).astype(BF16)


def position_dft(tab, z, norm, tm, tk):
    batch, k2, d = z.shape
    seq = tab.shape[0]
    return pl.pallas_call(
        functools.partial(_pos_dft_kernel, norm=norm),
        grid=(batch, seq // tm, k2 // tk),
        in_specs=[pl.BlockSpec((tm, tk), lambda b, i, k: (i, k)),
                  pl.BlockSpec((None, tk, d), lambda b, i, k: (b, k, 0))],
        out_specs=pl.BlockSpec((None, tm, d), lambda b, i, k: (b, i, 0)),
        out_shape=jax.ShapeDtypeStruct((batch, seq, d), BF16),
        scratch_shapes=[pltpu.VMEM((tm, d), F32)],
        compiler_params=_cparams(("parallel", "parallel", "arbitrary")),
        name="fnet_position_dft",
    )(tab, z)


def _swiglu(hg, hu):
    g = jnp.minimum(hg, SWIGLU_LIMIT)
    l = jnp.clip(hu, -SWIGLU_LIMIT, SWIGLU_LIMIT)
    return g * (1.0 / (1.0 + jnp.exp(-SWIGLU_ALPHA * g))) * (l + 1.0)


def _expert_kernel(ue_ref, ustart_ref, ulen_ref, route_ref,
                   hp_hbm, wg_ref, bg_ref, wu_ref, bu_ref, wd_ref, bd_ref,
                   ys_hbm,
                   stage, xb, wgb, wub, wdb, act, ybuf, sem_g, sem_s,
                   *, rows, blocks, nf, tf, n_tok, nb):
    u = pl.program_id(0)
    s = pl.program_id(1)
    ulen = ulen_ref[u]
    groups_per_step = rows // (nf * MOE_GROUP)
    half = xb.shape[1] // 2
    n_assign = TOP_K * n_tok

    def rows_in(n):
        cover = jnp.int32(0)
        for start, size in reversed(blocks):
            cover = jnp.where(n > start, jnp.maximum(cover, start + size), cover)
        return cover

    def rows_out(n):
        return (n + (MOE_GROUP - 1)) & ~(MOE_GROUP - 1)

    def row_in(base, i):
        tok = route_ref[base + i] & (ROUTE_TOK - 1)
        return pltpu.make_async_copy(hp_hbm.at[pl.ds(pl.multiple_of(tok * nb, nb), nb), :],
                                     stage.at[pl.ds(pl.multiple_of(i * nb, nb), nb), :], sem_g)

    def row_out(base, i):
        dest = lax.shift_right_logical(route_ref[base + i], ROUTE_SHIFT)
        return pltpu.make_async_copy(ybuf.at[pl.ds(pl.multiple_of(i * nb, nb), nb), :],
                                     ys_hbm.at[pl.ds(pl.multiple_of(dest * nb, nb), nb), :], sem_s)

    def loop_rows(n, fn):
        def single(i, c):
            fn(i)
            return c
        lax.fori_loop(0, n, single, 0)

    def for_groups(group_ids, n, fn):
        for g in group_ids:
            @pl.when(g * MOE_GROUP < n)
            def _():
                for i in range(g * MOE_GROUP, (g + 1) * MOE_GROUP):
                    fn(i)

    all_groups = range(rows // MOE_GROUP)
    prev = jnp.maximum(u - 1, 0)
    n_prev = jnp.where(u > 0, rows_out(ulen_ref[prev]), 0)

    @pl.when(jnp.logical_and(u == 0, s == 0))
    def _first_step():
        spare = pl.ds(0, MOE_SPARE * nb)
        ybuf[spare, :] = jnp.zeros((MOE_SPARE * nb, 128), jnp.uint32)
        zero_fill = pltpu.make_async_copy(ybuf.at[spare, :],
                                          ys_hbm.at[pl.ds(n_assign * nb, MOE_SPARE * nb), :], sem_s)
        zero_fill.start()
        zero_fill.wait()
        base = ustart_ref[0]
        loop_rows(rows_in(ulen), lambda i: row_in(base, i).start())

    @pl.when(jnp.logical_and(s == 0, ulen > 0))
    def _await_rows():
        base = ustart_ref[u]
        for_groups(all_groups, rows_in(ulen), lambda i: row_in(base, i).wait())
        for start, size in blocks:
            @pl.when(start < ulen)
            def _():
                r = pl.ds(start, size)
                for t in range(nb):
                    lo, hi = _unpack_pairs(_load_token_slab_block(stage, start, size, nb, t))
                    xb[r, t * 128:(t + 1) * 128] = lo.astype(BF16)
                    xb[r, half + t * 128:half + (t + 1) * 128] = hi.astype(BF16)

    @pl.when(jnp.logical_and(s < nf, ulen > 0))
    def _gate_up():
        base_next = ustart_ref[u + 1]
        n_next = rows_in(ulen_ref[u + 1])
        base_prev = ustart_ref[prev]
        for sv in range(nf):
            @pl.when(s == sv)
            def _():
                mine = range(sv * groups_per_step, (sv + 1) * groups_per_step)
                for_groups(mine, n_next, lambda i: row_in(base_next, i).start())
                for_groups(mine, n_prev, lambda i: row_out(base_prev, i).start())

        wgb[...] = wg_ref[...].astype(BF16)
        wub[...] = wu_ref[...].astype(BF16)
        for start, size in blocks:
            @pl.when(start < ulen)
            def _():
                r = pl.ds(start, size)
                x = xb[r, :]
                hg = jnp.dot(x, wgb[...], preferred_element_type=F32) + bg_ref[...]
                hu = jnp.dot(x, wub[...], preferred_element_type=F32) + bu_ref[...]
                act[s, r, :] = _swiglu(hg, hu).astype(BF16)

    @pl.when(jnp.logical_and(s == nf, ulen > 0))
    def _down():
        wdb[...] = wd_ref[...].astype(BF16)
        base_prev = ustart_ref[prev]
        for_groups(all_groups, n_prev, lambda i: row_out(base_prev, i).wait())
        for start, size in blocks:
            @pl.when(start < ulen)
            def _():
                r = pl.ds(start, size)
                y = bd_ref[...] + jnp.dot(act[0, r, :], wdb[0:tf, :], preferred_element_type=F32)
                for f in range(1, nf):
                    y = y + jnp.dot(act[f, r, :], wdb[f * tf:(f + 1) * tf, :], preferred_element_type=F32)
                _store_token_slabs(ybuf, start, _pack_pairs(y))

        @pl.when(ulen_ref[u + 1] == 0)
        def _():
            own = ustart_ref[u]
            loop_rows(rows_out(ulen), lambda i: row_out(own, i).start())
            loop_rows(rows_out(ulen), lambda i: row_out(own, i).wait())


def moe_experts(hp, layer, ue, ustart, ulen, route, wg, bg, wu, bu, wd, bd, rows, tf):
    nl, ne, d, fdim = wg.shape
    nb = d // 256
    n_tok = hp.shape[0] // nb
    nf = fdim // tf
    n_units = ue.shape[0] - 1
    last_f = nf - 1
    blocks = tuple((st, min(sz, rows - st)) for st, sz in MOE_BLOCKS if st < rows)
    assert rows % (nf * MOE_GROUP) == 0 and sum(sz for _, sz in blocks) == rows
    assert all(sz % (2 * MOE_GROUP) == 0 for _, sz in blocks) and MOE_GROUP <= MOE_SPARE <= rows
    assert n_tok <= ROUTE_TOK

    def f_idx(u, s, ulen_ref):
        return jnp.where(ulen_ref[u] > 0, jnp.minimum(s, last_f), last_f)

    w_in = pl.BlockSpec((None, None, d, tf),
                        lambda u, s, ue, us, ul, od: (layer, ue[u], 0, f_idx(u, s, ul)))
    b_in = pl.BlockSpec((None, None, 1, tf),
                        lambda u, s, ue, us, ul, od: (layer, ue[u], 0, f_idx(u, s, ul)))
    w_dn = pl.BlockSpec((None, None, fdim, d), lambda u, s, ue, us, ul, od: (layer, ue[u], 0, 0))
    b_dn = pl.BlockSpec((None, None, 1, d), lambda u, s, ue, us, ul, od: (layer, ue[u], 0, 0))
    any_spec = pl.BlockSpec(memory_space=pl.ANY)

    grid_spec = pltpu.PrefetchScalarGridSpec(
        num_scalar_prefetch=4,
        grid=(n_units, nf + 1),
        in_specs=[any_spec, w_in, b_in, w_in, b_in, w_dn, b_dn],
        out_specs=any_spec,
        scratch_shapes=[
            pltpu.VMEM((rows * nb, 128), jnp.uint32),
            pltpu.VMEM((rows, d), BF16),
            pltpu.VMEM((d, tf), BF16),
            pltpu.VMEM((d, tf), BF16),
            pltpu.VMEM((fdim, d), BF16),
            pltpu.VMEM((nf, rows, tf), BF16),
            pltpu.VMEM((rows * nb, 128), jnp.uint32),
            pltpu.SemaphoreType.DMA(()),
            pltpu.SemaphoreType.DMA(()),
        ],
    )
    return pl.pallas_call(
        functools.partial(_expert_kernel, rows=rows, blocks=blocks, nf=nf, tf=tf, n_tok=n_tok, nb=nb),
        grid_spec=grid_spec,
        out_shape=jax.ShapeDtypeStruct(((n_tok * TOP_K + MOE_SPARE) * nb, 128), jnp.uint32),
        compiler_params=_cparams(("arbitrary", "arbitrary")),
        name="moe_experts",
    )(ue, ustart, ulen, route, hp, wg, bg.reshape(nl, ne, 1, fdim), wu, bu.reshape(nl, ne, 1, fdim),
      wd, bd.reshape(nl, ne, 1, d))


def routing_tables(idx_t, rows):
    k, n = idx_t.shape
    n_assign = k * n
    flat_e = idx_t.T.reshape(-1)
    order = jnp.argsort(flat_e, stable=True).astype(jnp.int32)
    dest = (order % k) * n + order // k
    spare = n_assign + jnp.arange(rows, dtype=jnp.int32) % MOE_SPARE
    route = jnp.concatenate([dest * ROUTE_TOK + order // k, spare * ROUTE_TOK])
    eids = jnp.arange(N_EXPERTS, dtype=jnp.int32)
    counts = jnp.sum(flat_e[:, None] == eids[None, :], axis=0, dtype=jnp.int32)
    cstart = jnp.cumsum(counts) - counts
    nu = (counts + rows - 1) // rows
    ucum = jnp.cumsum(nu)
    per = (counts + jnp.maximum(nu, 1) - 1) // jnp.maximum(nu, 1)
    per = ((per + 7) // 8) * 8
    n_units = n_assign // rows + N_EXPERTS
    uid = jnp.arange(n_units + 1, dtype=jnp.int32)
    valid = uid < ucum[-1]
    e_last = jnp.sum(ucum < ucum[-1], dtype=jnp.int32)
    e_u = jnp.where(valid, jnp.sum(uid[:, None] >= ucum[None, :], axis=1, dtype=jnp.int32), e_last)
    onehot = (e_u[:, None] == eids[None, :]).astype(jnp.int32)

    def of_unit(v):
        return jnp.sum(onehot * v[None, :], axis=1)

    j = uid - of_unit(ucum - nu)
    per_u = of_unit(per)
    ustart = jnp.where(valid, of_unit(cstart) + j * per_u, 0)
    ulen = jnp.where(valid, jnp.clip(of_unit(counts) - j * per_u, 0, per_u), 0)
    return e_u, ustart, ulen, route


def _combine_ln_kernel(y0, y1, y2, y3, gt_ref, h_ref, g_ref, b_ref, o_ref, hp_ref, f_ref, *, nb):
    tm, d = h_ref.shape
    half = d // 2
    gt = gt_ref[...]
    slots = (y0, y1, y2, y3)
    for t in range(nb):
        lo_sum = hi_sum = None
        for kk, y_ref in enumerate(slots):
            lo, hi = _unpack_pairs(_load_token_slab_block(y_ref, 0, tm, nb, t))
            gk = gt[:, kk:kk + 1]
            lo_sum = lo * gk if lo_sum is None else lo_sum + lo * gk
            hi_sum = hi * gk if hi_sum is None else hi_sum + hi * gk
        f_ref[:, t * 128:(t + 1) * 128] = lo_sum
        f_ref[:, half + t * 128:half + (t + 1) * 128] = hi_sum
    o = _layer_norm(DEEPNORM_ALPHA * h_ref[...] + f_ref[...], g_ref[...], b_ref[...])
    o_ref[...] = o
    _store_token_slabs(hp_ref, 0, _pack_pairs(o))


def combine_residual_ln(ys, gates, h, g, b, tm=256):
    n, d = h.shape
    nb = d // 256
    nblk = n // tm
    vec = pl.BlockSpec((1, d), lambda i: (0, 0))

    def slot(kk):
        return pl.BlockSpec((tm * nb, 128), lambda i: (kk * nblk + i, 0))

    return pl.pallas_call(
        functools.partial(_combine_ln_kernel, nb=nb),
        grid=(nblk,),
        in_specs=[slot(0), slot(1), slot(2), slot(3),
                  pl.BlockSpec((tm, TOP_K), lambda i: (i, 0)),
                  pl.BlockSpec((tm, d), lambda i: (i, 0)), vec, vec],
        out_specs=[pl.BlockSpec((tm, d), lambda i: (i, 0)),
                   pl.BlockSpec((tm * nb, 128), lambda i: (i, 0))],
        out_shape=[jax.ShapeDtypeStruct((n, d), F32),
                   jax.ShapeDtypeStruct((n * nb, 128), jnp.uint32)],
        scratch_shapes=[pltpu.VMEM((tm, d), F32)],
        compiler_params=_cparams(("parallel",)),
        name="moe_combine_ln",
    )(ys, ys, ys, ys, gates, h, g.reshape(1, d), b.reshape(1, d))


def _prep_mla(w_dq, q_norm, w_uq, w_dkv, kv_norm, w_ukv, w_o):
    nh = N_HEADS
    d = w_dq.shape[0]
    wuq = w_uq.reshape(Q_LORA, nh, QK_NOPE + QK_ROPE)
    pe = wuq[:, :, QK_NOPE:]
    z = jnp.zeros((Q_LORA, nh, QK_ROPE // 2), F32)
    wuq_p = jnp.concatenate([pe[:, :, 0::2], z, pe[:, :, 1::2], z], axis=-1)
    kpe = w_dkv[:, KV_LORA:]
    zk = jnp.zeros((d, QK_ROPE // 2), F32)
    wdkv = jnp.concatenate([w_dkv[:, :KV_LORA], kpe[:, 0::2], zk, kpe[:, 1::2], zk], axis=1)
    wukv = w_ukv.reshape(KV_LORA, nh, QK_NOPE + V_HEAD)
    return {
        "wdq": w_dq.astype(BF16),
        "qn": q_norm.reshape(1, Q_LORA),
        "wuq_n": wuq[:, :, :QK_NOPE].reshape(Q_LORA, nh * 128).astype(BF16),
        "wuq_p": wuq_p.reshape(Q_LORA, nh * 128).astype(BF16),
        "wdkv": wdkv.astype(BF16),
        "kvn": kv_norm.reshape(1, KV_LORA),
        "wuk": wukv[:, :, :QK_NOPE].reshape(KV_LORA, nh * 128).astype(BF16),
        "wuvt": wukv[:, :, QK_NOPE:].reshape(KV_LORA, nh * V_HEAD).T.astype(BF16),
        "wo": w_o.astype(BF16),
    }


def _rope_tables(positions):
    inv_freq = 1.0 / (ROPE_THETA ** (jnp.arange(0, QK_ROPE, 2, dtype=F32) / QK_ROPE))
    ang = positions.astype(F32).reshape(-1)[:, None] * inv_freq
    c, s = jnp.cos(ang), jnp.sin(ang)
    z = jnp.zeros_like(c)
    return jnp.concatenate([c, z, c, z], axis=1), jnp.concatenate([-s, z, s, z], axis=1)


def _dft_tables(seq, gd, tk):
    def cs(n):
        i = jnp.arange(n, dtype=jnp.int32)
        ang = ((i[:, None] * i[None, :]) % n).astype(F32) * (2.0 * math.pi / n)
        return jnp.cos(ang), jnp.sin(ang)

    cc, sc = cs(gd)
    chan = jnp.concatenate([cc, sc], axis=1).astype(BF16)

    a = 1 << ((seq.bit_length() - 1) // 2)
    col = jnp.arange(2 * seq, dtype=jnp.int32)
    within = col % (2 * tk)
    key = (col // (2 * tk)) * tk + within % tk
    quarter = (within >= tk).astype(jnp.int32) * (seq // 4)
    j1 = jnp.arange(seq // a, dtype=jnp.int32)[:, None]
    j0 = jnp.arange(a, dtype=jnp.int32)[:, None]
    ang_a = ((a * j1 * key[None, :]) % seq).astype(F32) * (2.0 * math.pi / seq)
    ang_b = ((j0 * key[None, :] + quarter[None, :]) % seq).astype(F32) * (2.0 * math.pi / seq)
    pos = (jnp.cos(ang_a)[:, None, :] * jnp.cos(ang_b)[None, :, :]
           - jnp.sin(ang_a)[:, None, :] * jnp.sin(ang_b)[None, :, :])
    return chan, pos.reshape(seq, 2 * seq).astype(BF16)


def kernel(x, positions, ln_in_g, ln_in_b, mla_w_dq, mla_q_norm, mla_w_uq, mla_w_dkv, mla_kv_norm,
           mla_w_ukv, mla_w_o, fnet_w, fnet_b, ln_mix_g, ln_mix_b, router_w, router_b, exp_w_gate,
           exp_b_gate, exp_w_up, exp_b_up, exp_w_down, exp_b_down, ln_moe_g, ln_moe_b):
    batch, seq, d = x.shape
    n = batch * seq
    t_seq = 512
    moe_tf = 256

    rope_c, rope_s = _rope_tables(positions)
    gd = d // FNET_GROUPS
    chan_tab, pos_tab = _dft_tables(seq, gd, t_seq)
    dft_norm = 1.0 / math.sqrt(seq * gd)
    zero_bias = jnp.zeros((d,), F32)

    h = layer_norm_rows(x.reshape(n, d), ln_in_g, ln_in_b)
    for i in range(DEPTH):
        j = i // 2
        if i % 2 == 0:
            w = _prep_mla(mla_w_dq[j], mla_q_norm[j], mla_w_uq[j], mla_w_dkv[j], mla_kv_norm[j],
                          mla_w_ukv[j], mla_w_o[j])
            q, k, vt = mla_projections(h, rope_c, rope_s, w, batch, seq, t_seq)
            o = attention(q, k, vt, tq=min(seq, 4096))
            h, hp, idx_t, gate_t = proj_residual_ln(
                o.reshape(n, N_HEADS * V_HEAD), w["wo"], zero_bias, h, ln_mix_g[i], ln_mix_b[i],
                router_w[i].T, router_b[i])
        else:
            z = channel_dft(h, chan_tab, t_seq).reshape(batch, 2 * seq, d)
            mixed = position_dft(pos_tab, z, dft_norm, tm=1024, tk=2 * t_seq)
            h, hp, idx_t, gate_t = proj_residual_ln(
                mixed.reshape(n, d), fnet_w[j].astype(BF16), fnet_b[j], h, ln_mix_g[i], ln_mix_b[i],
                router_w[i].T, router_b[i])
        ue, ustart, ulen, route = routing_tables(idx_t, MOE_ROWS)
        ys = moe_experts(hp, i, ue, ustart, ulen, route, exp_w_gate, exp_b_gate, exp_w_up,
                         exp_b_up, exp_w_down, exp_b_down, MOE_ROWS, moe_tf)
        h, _ = combine_residual_ln(ys, gate_t.T, h, ln_moe_g[i], ln_moe_b[i])
    return h.reshape(batch, seq, d)
```

```python
import functools
import math

import jax
import jax.numpy as jnp
from jax import lax
from jax.experimental import pallas as pl
from jax.experimental.pallas import tpu as pltpu

F32 = jnp.float32
BF16 = jnp.bfloat16

DEPTH = 4
N_HEADS = 16
Q_LORA = 512
KV_LORA = 512
QK_NOPE = 128
QK_ROPE = 64
V_HEAD = 128
ROPE_THETA = 10000.0
FNET_GROUPS = 4
N_EXPERTS = 32
TOP_K = 4
SWIGLU_LIMIT = 7.0
SWIGLU_ALPHA = 1.702
DEEPNORM_ALPHA = (2 * DEPTH) ** 0.25
LN_EPS = 1e-5
RMS_EPS = 1e-6

HEAD_PAD = 256
VMEM_LIMIT = 56 * 1024 * 1024
MOE_ROWS = 1280
MOE_BLOCKS = ((0, 768), (768, 256), (1024, 128), (1152, 128))
MOE_GROUP = 64
MOE_SPARE = 256
ROUTE_SHIFT = 13
ROUTE_TOK = 1 << ROUTE_SHIFT


def _cparams(sem):
    return pltpu.CompilerParams(dimension_semantics=sem, vmem_limit_bytes=VMEM_LIMIT)


def _layer_norm(y, g, b):
    mu = jnp.mean(y, axis=-1, keepdims=True)
    yc = y - mu
    var = jnp.mean(yc * yc, axis=-1, keepdims=True)
    return yc * lax.rsqrt(var + LN_EPS) * g + b


def _rms_norm(y, g):
    ms = jnp.mean(y * y, axis=-1, keepdims=True)
    return y * lax.rsqrt(ms + RMS_EPS) * g


def _pack_pairs(y):
    w = y.shape[1] // 2
    lo = lax.bitcast_convert_type(y[:, :w].astype(BF16).astype(F32), jnp.uint32)
    hi = lax.bitcast_convert_type(y[:, w:].astype(BF16).astype(F32), jnp.uint32)
    return lax.shift_right_logical(lo, jnp.uint32(16)) | (hi & jnp.uint32(0xFFFF0000))


def _unpack_pairs(p):
    lo = lax.bitcast_convert_type(lax.shift_left(p, jnp.uint32(16)), F32)
    hi = lax.bitcast_convert_type(p & jnp.uint32(0xFFFF0000), F32)
    return lo, hi


def _store_token_slabs(ref, row0, packed):
    rows, w = packed.shape
    nb = w // 128
    for t in range(nb):
        ref[pl.ds(row0 * nb + t, rows, stride=nb), :] = packed[:, t * 128:(t + 1) * 128]


def _load_token_slab_block(ref, row0, rows, nb, t):
    return ref[pl.ds(row0 * nb + t, rows, stride=nb), :]


def _ln_kernel(x_ref, g_ref, b_ref, o_ref):
    o_ref[...] = _layer_norm(x_ref[...], g_ref[...], b_ref[...])


def layer_norm_rows(x, g, b, tm=512):
    n, d = x.shape
    row = pl.BlockSpec((tm, d), lambda i: (i, 0))
    vec = pl.BlockSpec((1, d), lambda i: (0, 0))
    return pl.pallas_call(
        _ln_kernel,
        grid=(n // tm,),
        in_specs=[row, vec, vec],
        out_specs=row,
        out_shape=jax.ShapeDtypeStruct((n, d), F32),
        compiler_params=_cparams(("parallel",)),
        name="ln_in",
    )(x, g.reshape(1, d), b.reshape(1, d))


def _rope(x, c, s):
    return x * c + pltpu.roll(x, 64, 1) * s


def _q_proj_kernel(h_ref, c_ref, s_ref, wdq_ref, qn_ref, wn_ref, wp_ref, q_ref, *, n_heads, q_scale):
    x = h_ref[...].astype(BF16)
    qa = jnp.dot(x, wdq_ref[...], preferred_element_type=F32)
    cq = _rms_norm(qa, qn_ref[...]).astype(BF16)
    qn = jnp.dot(cq, wn_ref[...], preferred_element_type=F32) * q_scale
    qp = jnp.dot(cq, wp_ref[...], preferred_element_type=F32) * q_scale
    c = c_ref[...]
    s = s_ref[...]
    for hd in range(n_heads):
        sl = slice(hd * 128, (hd + 1) * 128)
        q_ref[hd, :, 0:128] = qn[:, sl].astype(BF16)
        q_ref[hd, :, 128:256] = _rope(qp[:, sl], c, s).astype(BF16)


def _kv_proj_kernel(h_ref, c_ref, s_ref, wdkv_ref, kvn_ref, wk_ref, wvt_ref, k_ref, vt_ref, *, n_heads, kv_lora):
    x = h_ref[...].astype(BF16)
    kva = jnp.dot(x, wdkv_ref[...], preferred_element_type=F32)
    ckv = _rms_norm(kva[:, :kv_lora], kvn_ref[...]).astype(BF16)
    kpe = _rope(kva[:, kv_lora:], c_ref[...], s_ref[...]).astype(BF16)
    kn = jnp.dot(ckv, wk_ref[...], preferred_element_type=F32)
    vt = lax.dot_general(wvt_ref[...], ckv, (((1,), (1,)), ((), ())),
                         preferred_element_type=F32)
    for hd in range(n_heads):
        sl = slice(hd * 128, (hd + 1) * 128)
        k_ref[hd, :, 0:128] = kn[:, sl].astype(BF16)
        k_ref[hd, :, 128:256] = kpe
        vt_ref[hd] = vt[sl, :].astype(BF16)


def mla_projections(h, rope_c, rope_s, w, batch, seq, tm):
    n, d = h.shape
    nh = N_HEADS
    nt = seq // tm
    q_scale = (QK_NOPE + QK_ROPE) ** -0.5 * math.log2(math.e)

    row = pl.BlockSpec((tm, d), lambda b, i: (b * nt + i, 0))
    tab = pl.BlockSpec((tm, 128), lambda b, i: (b * nt + i, 0))

    def full(a):
        return pl.BlockSpec(a.shape, lambda b, i: (0,) * a.ndim)

    head_out = pl.BlockSpec((None, nh, tm, HEAD_PAD), lambda b, i: (b, 0, i, 0))
    q = pl.pallas_call(
        functools.partial(_q_proj_kernel, n_heads=nh, q_scale=q_scale),
        grid=(batch, nt),
        in_specs=[row, tab, tab, full(w["wdq"]), full(w["qn"]), full(w["wuq_n"]), full(w["wuq_p"])],
        out_specs=head_out,
        out_shape=jax.ShapeDtypeStruct((batch, nh, seq, HEAD_PAD), BF16),
        compiler_params=_cparams(("parallel", "parallel")),
        name="mla_q_proj",
    )(h, rope_c, rope_s, w["wdq"], w["qn"], w["wuq_n"], w["wuq_p"])

    k, vt = pl.pallas_call(
        functools.partial(_kv_proj_kernel, n_heads=nh, kv_lora=KV_LORA),
        grid=(batch, nt),
        in_specs=[row, tab, tab, full(w["wdkv"]), full(w["kvn"]), full(w["wuk"]), full(w["wuvt"])],
        out_specs=[head_out,
                   pl.BlockSpec((None, nh, None, V_HEAD, tm), lambda b, i: (b, 0, i, 0, 0))],
        out_shape=[jax.ShapeDtypeStruct((batch, nh, seq, HEAD_PAD), BF16),
                   jax.ShapeDtypeStruct((batch, nh, nt, V_HEAD, tm), BF16)],
        compiler_params=_cparams(("parallel", "parallel")),
        name="mla_kv_proj",
    )(h, rope_c, rope_s, w["wdkv"], w["kvn"], w["wuk"], w["wuvt"])
    return q, k, vt


def _attn_kernel(q_ref, k_ref, vt_ref, o_ref, s_a, s_b, p_a, p_b, *, tq, tk):
    seq = q_ref.shape[0]
    nq = seq // tq
    nk = seq // tk
    nt = (((1,), (1,)), ((), ()))

    def q_body(qi, carry):
        q0 = pl.multiple_of(qi * tq, tq)
        q = q_ref[pl.ds(q0, tq), :]

        def scores(j):
            k0 = pl.multiple_of(j * tk, tk)
            return lax.dot_general(k_ref[pl.ds(k0, tk), :], q, nt, preferred_element_type=F32)

        def stage(j, s_cur, s_next, p_cur, p_prev, st, first=False, last=False):
            m, l, acc, alpha_prev = st
            if not last:
                s_next[...] = scores(j + 1)
            sc = s_cur[...]
            m_new = jnp.maximum(m, jnp.max(sc, axis=0, keepdims=True))
            alpha = jnp.exp2(m - m_new)
            p = jnp.exp2(sc - m_new)
            l = alpha * l + jnp.sum(p, axis=0, keepdims=True)
            if not first:
                acc = alpha_prev * acc + jnp.dot(vt_ref[j - 1], p_prev[...], preferred_element_type=F32)
            p_cur[...] = p.astype(BF16)
            return m_new, l, acc, alpha

        def pair(jj, st):
            st = stage(2 * jj + 1, s_b, s_a, p_b, p_a, st)
            return stage(2 * jj + 2, s_a, s_b, p_a, p_b, st)

        s_a[...] = scores(0)
        st = (jnp.full((1, tq), -1e30, F32), jnp.zeros((1, tq), F32),
              jnp.zeros((V_HEAD, tq), F32), jnp.ones((1, tq), F32))
        st = stage(0, s_a, s_b, p_a, p_b, st, first=True)
        st = lax.fori_loop(0, (nk - 2) // 2, pair, st)
        _, l, acc, alpha_last = stage(nk - 1, s_b, s_a, p_b, p_a, st, last=True)
        acc = alpha_last * acc + jnp.dot(vt_ref[nk - 1], p_b[...], preferred_element_type=F32)
        o_ref[pl.ds(q0, tq), :] = (acc / l).T.astype(BF16)
        return carry

    lax.fori_loop(0, nq, q_body, 0)


def attention(q, k, vt, tq):
    batch, nh, seq, _ = q.shape
    nt, tk = vt.shape[2], vt.shape[4]
    assert nt % 2 == 0
    return pl.pallas_call(
        functools.partial(_attn_kernel, tq=tq, tk=tk),
        grid=(batch, nh),
        in_specs=[pl.BlockSpec((None, None, seq, HEAD_PAD), lambda b, h: (b, h, 0, 0)),
                  pl.BlockSpec((None, None, seq, HEAD_PAD), lambda b, h: (b, h, 0, 0)),
                  pl.BlockSpec((None, None, nt, V_HEAD, tk), lambda b, h: (b, h, 0, 0, 0))],
        out_specs=pl.BlockSpec((None, seq, V_HEAD), lambda b, h: (b, 0, h)),
        out_shape=jax.ShapeDtypeStruct((batch, seq, nh * V_HEAD), BF16),
        scratch_shapes=[pltpu.VMEM((tk, tq), F32), pltpu.VMEM((tk, tq), F32),
                        pltpu.VMEM((tk, tq), BF16), pltpu.VMEM((tk, tq), BF16)],
        compiler_params=_cparams(("parallel", "parallel")),
        name="mla_attention",
    )(q, k, vt)


def _split_bf16(x):
    hi = x.astype(BF16)
    lo = (x - hi.astype(F32)).astype(BF16)
    return hi, lo


def _route(x, w_t, bias, top_k):
    nt = (((1,), (1,)), ((), ()))
    x_hi, x_lo = _split_bf16(x)
    w_hi, w_lo = _split_bf16(w_t)
    logits = (lax.dot_general(w_hi, x_hi, nt, preferred_element_type=F32)
              + lax.dot_general(w_hi, x_lo, nt, preferred_element_type=F32)
              + lax.dot_general(w_lo, x_hi, nt, preferred_element_type=F32)
              + bias)
    ne = logits.shape[0]
    eid = lax.broadcasted_iota(jnp.int32, logits.shape, 0)
    vals, idxs = [], []
    cur = logits
    for _ in range(top_k):
        mx = jnp.max(cur, axis=0, keepdims=True)
        ix = jnp.min(jnp.where(cur == mx, eid, ne), axis=0, keepdims=True)
        vals.append(mx)
        idxs.append(ix)
        cur = jnp.where(eid == ix, -jnp.inf, cur)
    ex = [jnp.exp(v - vals[0]) for v in vals]
    tot = ex[0]
    for e in ex[1:]:
        tot = tot + e
    return idxs, [e / tot for e in ex]


def _proj_ln_kernel(a_ref, w_ref, bias_ref, h_ref, g_ref, b_ref, wr_ref, br_ref,
                    o_ref, hp_ref, idx_ref, gate_ref):
    m = jnp.dot(a_ref[...], w_ref[...], preferred_element_type=F32) + bias_ref[...]
    o = _layer_norm(DEEPNORM_ALPHA * h_ref[...] + m, g_ref[...], b_ref[...])
    o_ref[...] = o
    _store_token_slabs(hp_ref, 0, _pack_pairs(o))
    idxs, gates = _route(o, wr_ref[...], br_ref[...], TOP_K)
    for kk in range(TOP_K):
        idx_ref[kk:kk + 1, :] = idxs[kk]
        gate_ref[kk:kk + 1, :] = gates[kk]


def proj_residual_ln(a, w, bias, h, g, b, wr_t, br, tm=512):
    n, d = h.shape
    kdim = a.shape[1]
    ne = wr_t.shape[0]
    nb = d // 256
    vec = pl.BlockSpec((1, d), lambda i: (0, 0))
    return pl.pallas_call(
        _proj_ln_kernel,
        grid=(n // tm,),
        in_specs=[pl.BlockSpec((tm, kdim), lambda i: (i, 0)),
                  pl.BlockSpec((kdim, d), lambda i: (0, 0)),
                  vec,
                  pl.BlockSpec((tm, d), lambda i: (i, 0)),
                  vec, vec,
                  pl.BlockSpec((ne, d), lambda i: (0, 0)),
                  pl.BlockSpec((ne, 1), lambda i: (0, 0))],
        out_specs=[pl.BlockSpec((tm, d), lambda i: (i, 0)),
                   pl.BlockSpec((tm * nb, 128), lambda i: (i, 0)),
                   pl.BlockSpec((TOP_K, tm), lambda i: (0, i)),
                   pl.BlockSpec((TOP_K, tm), lambda i: (0, i))],
        out_shape=[jax.ShapeDtypeStruct((n, d), F32),
                   jax.ShapeDtypeStruct((n * nb, 128), jnp.uint32),
                   jax.ShapeDtypeStruct((TOP_K, n), jnp.int32),
                   jax.ShapeDtypeStruct((TOP_K, n), F32)],
        compiler_params=_cparams(("parallel",)),
        name="proj_residual_ln",
    )(a, w, bias.reshape(1, d), h, g.reshape(1, d), b.reshape(1, d), wr_t, br.reshape(ne, 1))


def _chan_dft_kernel(h_ref, t_ref, z_ref, *, groups, gd):
    t = t_ref[...]
    for g in range(groups):
        sl = slice(g * gd, (g + 1) * gd)
        xg = h_ref[:, sl].astype(BF16)
        zz = jnp.dot(xg, t, preferred_element_type=F32)
        z_ref[0, :, sl] = zz[:, :gd].astype(BF16)
        z_ref[1, :, sl] = zz[:, gd:].astype(BF16)


def channel_dft(h, tab, tm):
    n, d = h.shape
    gd = d // FNET_GROUPS
    return pl.pallas_call(
        functools.partial(_chan_dft_kernel, groups=FNET_GROUPS, gd=gd),
        grid=(n // tm,),
        in_specs=[pl.BlockSpec((tm, d), lambda i: (i, 0)),
                  pl.BlockSpec((gd, 2 * gd), lambda i: (0, 0))],
        out_specs=pl.BlockSpec((None, 2, tm, d), lambda i: (i, 0, 0, 0)),
        out_shape=jax.ShapeDtypeStruct((n // tm, 2, tm, d), BF16),
        compiler_params=_cparams(("parallel",)),
        name="fnet_channel_dft",
    )(h, tab)


def _pos_dft_kernel(t_ref, z_ref, o_ref, acc_ref, *, norm):
    k = pl.program_id(2)

    @pl.when(k == 0)
    def _():
        acc_ref[...] = jnp.zeros_like(acc_ref)

    acc_ref[...] += jnp.dot(t_ref[...], z_ref[...], preferred_element_type=F32)

    @pl.when(k == pl.num_programs(2) - 1)
    def _():
        o_ref[...] = (acc_ref[...] * norm).astype(BF16)


def position_dft(tab, z, norm, tm, tk):
    batch, k2, d = z.shape
    seq = tab.shape[0]
    return pl.pallas_call(
        functools.partial(_pos_dft_kernel, norm=norm),
        grid=(batch, seq // tm, k2 // tk),
        in_specs=[pl.BlockSpec((tm, tk), lambda b, i, k: (i, k)),
                  pl.BlockSpec((None, tk, d), lambda b, i, k: (b, k, 0))],
        out_specs=pl.BlockSpec((None, tm, d), lambda b, i, k: (b, i, 0)),
        out_shape=jax.ShapeDtypeStruct((batch, seq, d), BF16),
        scratch_shapes=[pltpu.VMEM((tm, d), F32)],
        compiler_params=_cparams(("parallel", "parallel", "arbitrary")),
        name="fnet_position_dft",
    )(tab, z)


def _swiglu(hg, hu):
    g = jnp.minimum(hg, SWIGLU_LIMIT)
    l = jnp.clip(hu, -SWIGLU_LIMIT, SWIGLU_LIMIT)
    return g * (1.0 / (1.0 + jnp.exp(-SWIGLU_ALPHA * g))) * (l + 1.0)


def _expert_kernel(ue_ref, ustart_ref, ulen_ref, route_ref,
                   hp_hbm, wg_ref, bg_ref, wu_ref, bu_ref, wd_ref, bd_ref,
                   ys_hbm,
                   stage, xb, wgb, wub, wdb, act, ybuf, sem_g, sem_s,
                   *, rows, blocks, nf, tf, n_tok, nb):
    u = pl.program_id(0)
    s = pl.program_id(1)
    ulen = ulen_ref[u]
    groups_per_step = rows // (nf * MOE_GROUP)
    half = xb.shape[1] // 2
    n_assign = TOP_K * n_tok

    def rows_in(n):
        cover = jnp.int32(0)
        for start, size in reversed(blocks):
            cover = jnp.where(n > start, jnp.maximum(cover, start + size), cover)
        return cover

    def rows_out(n):
        return (n + (MOE_GROUP - 1)) & ~(MOE_GROUP - 1)

    def row_in(base, i):
        tok = route_ref[base + i] & (ROUTE_TOK - 1)
        return pltpu.make_async_copy(hp_hbm.at[pl.ds(pl.multiple_of(tok * nb, nb), nb), :],
                                     stage.at[pl.ds(pl.multiple_of(i * nb, nb), nb), :], sem_g)

    def row_out(base, i):
        dest = lax.shift_right_logical(route_ref[base + i], ROUTE_SHIFT)
        return pltpu.make_async_copy(ybuf.at[pl.ds(pl.multiple_of(i * nb, nb), nb), :],
                                     ys_hbm.at[pl.ds(pl.multiple_of(dest * nb, nb), nb), :], sem_s)

    def loop_rows(n, fn):
        def single(i, c):
            fn(i)
            return c
        lax.fori_loop(0, n, single, 0)

    def for_groups(group_ids, n, fn):
        for g in group_ids:
            @pl.when(g * MOE_GROUP < n)
            def _():
                for i in range(g * MOE_GROUP, (g + 1) * MOE_GROUP):
                    fn(i)

    all_groups = range(rows // MOE_GROUP)
    prev = jnp.maximum(u - 1, 0)
    n_prev = jnp.where(u > 0, rows_out(ulen_ref[prev]), 0)

    @pl.when(jnp.logical_and(u == 0, s == 0))
    def _first_step():
        spare = pl.ds(0, MOE_SPARE * nb)
        ybuf[spare, :] = jnp.zeros((MOE_SPARE * nb, 128), jnp.uint32)
        zero_fill = pltpu.make_async_copy(ybuf.at[spare, :],
                                          ys_hbm.at[pl.ds(n_assign * nb, MOE_SPARE * nb), :], sem_s)
        zero_fill.start()
        zero_fill.wait()
        base = ustart_ref[0]
        loop_rows(rows_in(ulen), lambda i: row_in(base, i).start())

    @pl.when(jnp.logical_and(s == 0, ulen > 0))
    def _await_rows():
        base = ustart_ref[u]
        for_groups(all_groups, rows_in(ulen), lambda i: row_in(base, i).wait())
        for start, size in blocks:
            @pl.when(start < ulen)
            def _():
                r = pl.ds(start, size)
                for t in range(nb):
                    lo, hi = _unpack_pairs(_load_token_slab_block(stage, start, size, nb, t))
                    xb[r, t * 128:(t + 1) * 128] = lo.astype(BF16)
                    xb[r, half + t * 128:half + (t + 1) * 128] = hi.astype(BF16)

    @pl.when(jnp.logical_and(s < nf, ulen > 0))
    def _gate_up():
        base_next = ustart_ref[u + 1]
        n_next = rows_in(ulen_ref[u + 1])
        base_prev = ustart_ref[prev]
        for sv in range(nf):
            @pl.when(s == sv)
            def _():
                mine = range(sv * groups_per_step, (sv + 1) * groups_per_step)
                for_groups(mine, n_next, lambda i: row_in(base_next, i).start())
                for_groups(mine, n_prev, lambda i: row_out(base_prev, i).start())

        wgb[...] = wg_ref[...].astype(BF16)
        wub[...] = wu_ref[...].astype(BF16)
        for start, size in blocks:
            @pl.when(start < ulen)
            def _():
                r = pl.ds(start, size)
                x = xb[r, :]
                hg = jnp.dot(x, wgb[...], preferred_element_type=F32) + bg_ref[...]
                hu = jnp.dot(x, wub[...], preferred_element_type=F32) + bu_ref[...]
                act[s, r, :] = _swiglu(hg, hu).astype(BF16)

    @pl.when(jnp.logical_and(s == nf, ulen > 0))
    def _down():
        wdb[...] = wd_ref[...].astype(BF16)
        base_prev = ustart_ref[prev]
        for_groups(all_groups, n_prev, lambda i: row_out(base_prev, i).wait())
        for start, size in blocks:
            @pl.when(start < ulen)
            def _():
                r = pl.ds(start, size)
                y = bd_ref[...] + jnp.dot(act[0, r, :], wdb[0:tf, :], preferred_element_type=F32)
                for f in range(1, nf):
                    y = y + jnp.dot(act[f, r, :], wdb[f * tf:(f + 1) * tf, :], preferred_element_type=F32)
                _store_token_slabs(ybuf, start, _pack_pairs(y))

        @pl.when(ulen_ref[u + 1] == 0)
        def _():
            own = ustart_ref[u]
            loop_rows(rows_out(ulen), lambda i: row_out(own, i).start())
            loop_rows(rows_out(ulen), lambda i: row_out(own, i).wait())


def moe_experts(hp, layer, ue, ustart, ulen, route, wg, bg, wu, bu, wd, bd, rows, tf):
    nl, ne, d, fdim = wg.shape
    nb = d // 256
    n_tok = hp.shape[0] // nb
    nf = fdim // tf
    n_units = ue.shape[0] - 1
    last_f = nf - 1
    blocks = tuple((st, min(sz, rows - st)) for st, sz in MOE_BLOCKS if st < rows)
    assert rows % (nf * MOE_GROUP) == 0 and sum(sz for _, sz in blocks) == rows
    assert all(sz % (2 * MOE_GROUP) == 0 for _, sz in blocks) and MOE_GROUP <= MOE_SPARE <= rows
    assert n_tok <= ROUTE_TOK

    def f_idx(u, s, ulen_ref):
        return jnp.where(ulen_ref[u] > 0, jnp.minimum(s, last_f), last_f)

    w_in = pl.BlockSpec((None, None, d, tf),
                        lambda u, s, ue, us, ul, od: (layer, ue[u], 0, f_idx(u, s, ul)))
    b_in = pl.BlockSpec((None, None, 1, tf),
                        lambda u, s, ue, us, ul, od: (layer, ue[u], 0, f_idx(u, s, ul)))
    w_dn = pl.BlockSpec((None, None, fdim, d), lambda u, s, ue, us, ul, od: (layer, ue[u], 0, 0))
    b_dn = pl.BlockSpec((None, None, 1, d), lambda u, s, ue, us, ul, od: (layer, ue[u], 0, 0))
    any_spec = pl.BlockSpec(memory_space=pl.ANY)

    grid_spec = pltpu.PrefetchScalarGridSpec(
        num_scalar_prefetch=4,
        grid=(n_units, nf + 1),
        in_specs=[any_spec, w_in, b_in, w_in, b_in, w_dn, b_dn],
        out_specs=any_spec,
        scratch_shapes=[
            pltpu.VMEM((rows * nb, 128), jnp.uint32),
            pltpu.VMEM((rows, d), BF16),
            pltpu.VMEM((d, tf), BF16),
            pltpu.VMEM((d, tf), BF16),
            pltpu.VMEM((fdim, d), BF16),
            pltpu.VMEM((nf, rows, tf), BF16),
            pltpu.VMEM((rows * nb, 128), jnp.uint32),
            pltpu.SemaphoreType.DMA(()),
            pltpu.SemaphoreType.DMA(()),
        ],
    )
    return pl.pallas_call(
        functools.partial(_expert_kernel, rows=rows, blocks=blocks, nf=nf, tf=tf, n_tok=n_tok, nb=nb),
        grid_spec=grid_spec,
        out_shape=jax.ShapeDtypeStruct(((n_tok * TOP_K + MOE_SPARE) * nb, 128), jnp.uint32),
        compiler_params=_cparams(("arbitrary", "arbitrary")),
        name="moe_experts",
    )(ue, ustart, ulen, route, hp, wg, bg.reshape(nl, ne, 1, fdim), wu, bu.reshape(nl, ne, 1, fdim),
      wd, bd.reshape(nl, ne, 1, d))


def routing_tables(idx_t, rows):
    k, n = idx_t.shape
    n_assign = k * n
    flat_e = idx_t.T.reshape(-1)
    order = jnp.argsort(flat_e, stable=True).astype(jnp.int32)
    dest = (order % k) * n + order // k
    spare = n_assign + jnp.arange(rows, dtype=jnp.int32) % MOE_SPARE
    route = jnp.concatenate([dest * ROUTE_TOK + order // k, spare * ROUTE_TOK])
    eids = jnp.arange(N_EXPERTS, dtype=jnp.int32)
    counts = jnp.sum(flat_e[:, None] == eids[None, :], axis=0, dtype=jnp.int32)
    cstart = jnp.cumsum(counts) - counts
    nu = (counts + rows - 1) // rows
    ucum = jnp.cumsum(nu)
    per = (counts + jnp.maximum(nu, 1) - 1) // jnp.maximum(nu, 1)
    per = ((per + 7) // 8) * 8
    n_units = n_assign // rows + N_EXPERTS
    uid = jnp.arange(n_units + 1, dtype=jnp.int32)
    valid = uid < ucum[-1]
    e_last = jnp.sum(ucum < ucum[-1], dtype=jnp.int32)
    e_u = jnp.where(valid, jnp.sum(uid[:, None] >= ucum[None, :], axis=1, dtype=jnp.int32), e_last)
    onehot = (e_u[:, None] == eids[None, :]).astype(jnp.int32)

    def of_unit(v):
        return jnp.sum(onehot * v[None, :], axis=1)

    j = uid - of_unit(ucum - nu)
    per_u = of_unit(per)
    ustart = jnp.where(valid, of_unit(cstart) + j * per_u, 0)
    ulen = jnp.where(valid, jnp.clip(of_unit(counts) - j * per_u, 0, per_u), 0)
    return e_u, ustart, ulen, route


def _combine_ln_kernel(y0, y1, y2, y3, gt_ref, h_ref, g_ref, b_ref, o_ref, f_ref, *, nb):
    tm, d = h_ref.shape
    half = d // 2
    gt = gt_ref[...]
    slots = (y0, y1, y2, y3)
    for t in range(nb):
        lo_sum = hi_sum = None
        for kk, y_ref in enumerate(slots):
            lo, hi = _unpack_pairs(_load_token_slab_block(y_ref, 0, tm, nb, t))
            gk = gt[:, kk:kk + 1]
            lo_sum = lo * gk if lo_sum is None else lo_sum + lo * gk
            hi_sum = hi * gk if hi_sum is None else hi_sum + hi * gk
        f_ref[:, t * 128:(t + 1) * 128] = lo_sum
        f_ref[:, half + t * 128:half + (t + 1) * 128] = hi_sum
    o_ref[...] = _layer_norm(DEEPNORM_ALPHA * h_ref[...] + f_ref[...], g_ref[...], b_ref[...])


def combine_residual_ln(ys, gates, h, g, b, tm=256):
    n, d = h.shape
    nb = d // 256
    nblk = n // tm
    vec = pl.BlockSpec((1, d), lambda i: (0, 0))

    def slot(kk):
        return pl.BlockSpec((tm * nb, 128), lambda i: (kk * nblk + i, 0))

    return pl.pallas_call(
        functools.partial(_combine_ln_kernel, nb=nb),
        grid=(nblk,),
        in_specs=[slot(0), slot(1), slot(2), slot(3),
                  pl.BlockSpec((tm, TOP_K), lambda i: (i, 0)),
                  pl.BlockSpec((tm, d), lambda i: (i, 0)), vec, vec],
        out_specs=pl.BlockSpec((tm, d), lambda i: (i, 0)),
        out_shape=jax.ShapeDtypeStruct((n, d), F32),
        scratch_shapes=[pltpu.VMEM((tm, d), F32)],
        compiler_params=_cparams(("parallel",)),
        name="moe_combine_ln",
    )(ys, ys, ys, ys, gates, h, g.reshape(1, d), b.reshape(1, d))


def _prep_mla(w_dq, q_norm, w_uq, w_dkv, kv_norm, w_ukv, w_o):
    nh = N_HEADS
    d = w_dq.shape[0]
    wuq = w_uq.reshape(Q_LORA, nh, QK_NOPE + QK_ROPE)
    pe = wuq[:, :, QK_NOPE:]
    z = jnp.zeros((Q_LORA, nh, QK_ROPE // 2), F32)
    wuq_p = jnp.concatenate([pe[:, :, 0::2], z, pe[:, :, 1::2], z], axis=-1)
    kpe = w_dkv[:, KV_LORA:]
    zk = jnp.zeros((d, QK_ROPE // 2), F32)
    wdkv = jnp.concatenate([w_dkv[:, :KV_LORA], kpe[:, 0::2], zk, kpe[:, 1::2], zk], axis=1)
    wukv = w_ukv.reshape(KV_LORA, nh, QK_NOPE + V_HEAD)
    return {
        "wdq": w_dq.astype(BF16),
        "qn": q_norm.reshape(1, Q_LORA),
        "wuq_n": wuq[:, :, :QK_NOPE].reshape(Q_LORA, nh * 128).astype(BF16),
        "wuq_p": wuq_p.reshape(Q_LORA, nh * 128).astype(BF16),
        "wdkv": wdkv.astype(BF16),
        "kvn": kv_norm.reshape(1, KV_LORA),
        "wuk": wukv[:, :, :QK_NOPE].reshape(KV_LORA, nh * 128).astype(BF16),
        "wuvt": wukv[:, :, QK_NOPE:].reshape(KV_LORA, nh * V_HEAD).T.astype(BF16),
        "wo": w_o.astype(BF16),
    }


def _rope_tables(positions):
    inv_freq = 1.0 / (ROPE_THETA ** (jnp.arange(0, QK_ROPE, 2, dtype=F32) / QK_ROPE))
    ang = positions.astype(F32).reshape(-1)[:, None] * inv_freq
    c, s = jnp.cos(ang), jnp.sin(ang)
    z = jnp.zeros_like(c)
    return jnp.concatenate([c, z, c, z], axis=1), jnp.concatenate([-s, z, s, z], axis=1)


def _dft_tables(seq, gd, tk):
    def cs(n):
        i = jnp.arange(n, dtype=jnp.int32)
        ang = ((i[:, None] * i[None, :]) % n).astype(F32) * (2.0 * math.pi / n)
        return jnp.cos(ang), jnp.sin(ang)

    cc, sc = cs(gd)
    chan = jnp.concatenate([cc, sc], axis=1).astype(BF16)

    a = 1 << ((seq.bit_length() - 1) // 2)
    col = jnp.arange(2 * seq, dtype=jnp.int32)
    within = col % (2 * tk)
    key = (col // (2 * tk)) * tk + within % tk
    quarter = (within >= tk).astype(jnp.int32) * (seq // 4)
    j1 = jnp.arange(seq // a, dtype=jnp.int32)[:, None]
    j0 = jnp.arange(a, dtype=jnp.int32)[:, None]
    ang_a = ((a * j1 * key[None, :]) % seq).astype(F32) * (2.0 * math.pi / seq)
    ang_b = ((j0 * key[None, :] + quarter[None, :]) % seq).astype(F32) * (2.0 * math.pi / seq)
    pos = (jnp.cos(ang_a)[:, None, :] * jnp.cos(ang_b)[None, :, :]
           - jnp.sin(ang_a)[:, None, :] * jnp.sin(ang_b)[None, :, :])
    return chan, pos.reshape(seq, 2 * seq).astype(BF16)


def kernel(x, positions, ln_in_g, ln_in_b, mla_w_dq, mla_q_norm, mla_w_uq, mla_w_dkv, mla_kv_norm,
           mla_w_ukv, mla_w_o, fnet_w, fnet_b, ln_mix_g, ln_mix_b, router_w, router_b, exp_w_gate,
           exp_b_gate, exp_w_up, exp_b_up, exp_w_down, exp_b_down, ln_moe_g, ln_moe_b):
    batch, seq, d = x.shape
    n = batch * seq
    t_seq = 512
    moe_tf = 256

    rope_c, rope_s = _rope_tables(positions)
    gd = d // FNET_GROUPS
    chan_tab, pos_tab = _dft_tables(seq, gd, t_seq)
    dft_norm = 1.0 / math.sqrt(seq * gd)
    zero_bias = jnp.zeros((d,), F32)

    h = layer_norm_rows(x.reshape(n, d), ln_in_g, ln_in_b)
    for i in range(DEPTH):
        j = i // 2
        if i % 2 == 0:
            w = _prep_mla(mla_w_dq[j], mla_q_norm[j], mla_w_uq[j], mla_w_dkv[j], mla_kv_norm[j],
                          mla_w_ukv[j], mla_w_o[j])
            q, k, vt = mla_projections(h, rope_c, rope_s, w, batch, seq, t_seq)
            o = attention(q, k, vt, tq=min(seq, 4096))
            h, hp, idx_t, gate_t = proj_residual_ln(
                o.reshape(n, N_HEADS * V_HEAD), w["wo"], zero_bias, h, ln_mix_g[i], ln_mix_b[i],
                router_w[i].T, router_b[i])
        else:
            z = channel_dft(h, chan_tab, t_seq).reshape(batch, 2 * seq, d)
            mixed = position_dft(pos_tab, z, dft_norm, tm=1024, tk=2 * t_seq)
            h, hp, idx_t, gate_t = proj_residual_ln(
                mixed.reshape(n, d), fnet_w[j].astype(BF16), fnet_b[j], h, ln_mix_g[i], ln_mix_b[i],
                router_w[i].T, router_b[i])
        ue, ustart, ulen, route = routing_tables(idx_t, MOE_ROWS)
        ys = moe_experts(hp, i, ue, ustart, ulen, route, exp_w_gate, exp_b_gate, exp_w_up,
                         exp_b_up, exp_w_down, exp_b_down, MOE_ROWS, moe_tf)
        h = combine_residual_ln(ys, gate_t.T, h, ln_moe_g[i], ln_moe_b[i])
    return h.reshape(batch, seq, d)
```

```python
import functools
import math

import jax
import jax.numpy as jnp
from jax import lax
from jax.experimental import pallas as pl
from jax.experimental.pallas import tpu as pltpu

F32 = jnp.float32
BF16 = jnp.bfloat16

DEPTH = 4
N_HEADS = 16
Q_LORA = 512
KV_LORA = 512
QK_NOPE = 128
QK_ROPE = 64
V_HEAD = 128
ROPE_THETA = 10000.0
FNET_GROUPS = 4
N_EXPERTS = 32
TOP_K = 4
SWIGLU_LIMIT = 7.0
SWIGLU_ALPHA = 1.702
DEEPNORM_ALPHA = (2 * DEPTH) ** 0.25
LN_EPS = 1e-5
RMS_EPS = 1e-6

HEAD_PAD = 256
VMEM_LIMIT = 56 * 1024 * 1024
MOE_ROWS = 1280
MOE_BLOCKS = ((0, 768), (768, 256), (1024, 128), (1152, 128))
ROUTE_SHIFT = 13
ROUTE_TOK = 1 << ROUTE_SHIFT


def _cparams(sem):
    return pltpu.CompilerParams(dimension_semantics=sem, vmem_limit_bytes=VMEM_LIMIT)


def _layer_norm(y, g, b):
    mu = jnp.mean(y, axis=-1, keepdims=True)
    yc = y - mu
    var = jnp.mean(yc * yc, axis=-1, keepdims=True)
    return yc * lax.rsqrt(var + LN_EPS) * g + b


def _rms_norm(y, g):
    ms = jnp.mean(y * y, axis=-1, keepdims=True)
    return y * lax.rsqrt(ms + RMS_EPS) * g


def _pack_pairs(y):
    w = y.shape[1] // 2
    lo = lax.bitcast_convert_type(y[:, :w].astype(BF16).astype(F32), jnp.uint32)
    hi = lax.bitcast_convert_type(y[:, w:].astype(BF16).astype(F32), jnp.uint32)
    return lax.shift_right_logical(lo, jnp.uint32(16)) | (hi & jnp.uint32(0xFFFF0000))


def _unpack_pairs(p):
    lo = lax.bitcast_convert_type(lax.shift_left(p, jnp.uint32(16)), F32)
    hi = lax.bitcast_convert_type(p & jnp.uint32(0xFFFF0000), F32)
    return lo, hi


def _store_token_slabs(ref, row0, packed):
    rows, w = packed.shape
    nb = w // 128
    for t in range(nb):
        ref[pl.ds(row0 * nb + t, rows, stride=nb), :] = packed[:, t * 128:(t + 1) * 128]


def _load_token_slab_block(ref, row0, rows, nb, t):
    return ref[pl.ds(row0 * nb + t, rows, stride=nb), :]


def _ln_kernel(x_ref, g_ref, b_ref, o_ref):
    o_ref[...] = _layer_norm(x_ref[...], g_ref[...], b_ref[...])


def layer_norm_rows(x, g, b, tm=512):
    n, d = x.shape
    row = pl.BlockSpec((tm, d), lambda i: (i, 0))
    vec = pl.BlockSpec((1, d), lambda i: (0, 0))
    return pl.pallas_call(
        _ln_kernel,
        grid=(n // tm,),
        in_specs=[row, vec, vec],
        out_specs=row,
        out_shape=jax.ShapeDtypeStruct((n, d), F32),
        compiler_params=_cparams(("parallel",)),
        name="ln_in",
    )(x, g.reshape(1, d), b.reshape(1, d))


def _rope(x, c, s):
    return x * c + pltpu.roll(x, 64, 1) * s


def _q_proj_kernel(h_ref, c_ref, s_ref, wdq_ref, qn_ref, wn_ref, wp_ref, q_ref, *, n_heads, q_scale):
    x = h_ref[...].astype(BF16)
    qa = jnp.dot(x, wdq_ref[...], preferred_element_type=F32)
    cq = _rms_norm(qa, qn_ref[...]).astype(BF16)
    qn = jnp.dot(cq, wn_ref[...], preferred_element_type=F32) * q_scale
    qp = jnp.dot(cq, wp_ref[...], preferred_element_type=F32) * q_scale
    c = c_ref[...]
    s = s_ref[...]
    for hd in range(n_heads):
        sl = slice(hd * 128, (hd + 1) * 128)
        q_ref[hd, :, 0:128] = qn[:, sl].astype(BF16)
        q_ref[hd, :, 128:256] = _rope(qp[:, sl], c, s).astype(BF16)


def _kv_proj_kernel(h_ref, c_ref, s_ref, wdkv_ref, kvn_ref, wk_ref, wvt_ref, k_ref, vt_ref, *, n_heads, kv_lora):
    x = h_ref[...].astype(BF16)
    kva = jnp.dot(x, wdkv_ref[...], preferred_element_type=F32)
    ckv = _rms_norm(kva[:, :kv_lora], kvn_ref[...]).astype(BF16)
    kpe = _rope(kva[:, kv_lora:], c_ref[...], s_ref[...]).astype(BF16)
    kn = jnp.dot(ckv, wk_ref[...], preferred_element_type=F32)
    vt = lax.dot_general(wvt_ref[...], ckv, (((1,), (1,)), ((), ())),
                         preferred_element_type=F32)
    for hd in range(n_heads):
        sl = slice(hd * 128, (hd + 1) * 128)
        k_ref[hd, :, 0:128] = kn[:, sl].astype(BF16)
        k_ref[hd, :, 128:256] = kpe
        vt_ref[hd] = vt[sl, :].astype(BF16)


def mla_projections(h, rope_c, rope_s, w, batch, seq, tm):
    n, d = h.shape
    nh = N_HEADS
    nt = seq // tm
    q_scale = (QK_NOPE + QK_ROPE) ** -0.5 * math.log2(math.e)

    row = pl.BlockSpec((tm, d), lambda b, i: (b * nt + i, 0))
    tab = pl.BlockSpec((tm, 128), lambda b, i: (b * nt + i, 0))

    def full(a):
        return pl.BlockSpec(a.shape, lambda b, i: (0,) * a.ndim)

    head_out = pl.BlockSpec((None, nh, tm, HEAD_PAD), lambda b, i: (b, 0, i, 0))
    q = pl.pallas_call(
        functools.partial(_q_proj_kernel, n_heads=nh, q_scale=q_scale),
        grid=(batch, nt),
        in_specs=[row, tab, tab, full(w["wdq"]), full(w["qn"]), full(w["wuq_n"]), full(w["wuq_p"])],
        out_specs=head_out,
        out_shape=jax.ShapeDtypeStruct((batch, nh, seq, HEAD_PAD), BF16),
        compiler_params=_cparams(("parallel", "parallel")),
        name="mla_q_proj",
    )(h, rope_c, rope_s, w["wdq"], w["qn"], w["wuq_n"], w["wuq_p"])

    k, vt = pl.pallas_call(
        functools.partial(_kv_proj_kernel, n_heads=nh, kv_lora=KV_LORA),
        grid=(batch, nt),
        in_specs=[row, tab, tab, full(w["wdkv"]), full(w["kvn"]), full(w["wuk"]), full(w["wuvt"])],
        out_specs=[head_out,
                   pl.BlockSpec((None, nh, None, V_HEAD, tm), lambda b, i: (b, 0, i, 0, 0))],
        out_shape=[jax.ShapeDtypeStruct((batch, nh, seq, HEAD_PAD), BF16),
                   jax.ShapeDtypeStruct((batch, nh, nt, V_HEAD, tm), BF16)],
        compiler_params=_cparams(("parallel", "parallel")),
        name="mla_kv_proj",
    )(h, rope_c, rope_s, w["wdkv"], w["kvn"], w["wuk"], w["wuvt"])
    return q, k, vt


def _attn_kernel(q_ref, k_ref, vt_ref, o_ref, s_a, s_b, p_a, p_b, *, tq, tk):
    seq = q_ref.shape[0]
    nq = seq // tq
    nk = seq // tk
    nt = (((1,), (1,)), ((), ()))

    def q_body(qi, carry):
        q0 = pl.multiple_of(qi * tq, tq)
        q = q_ref[pl.ds(q0, tq), :]

        def scores(j):
            k0 = pl.multiple_of(j * tk, tk)
            return lax.dot_general(k_ref[pl.ds(k0, tk), :], q, nt, preferred_element_type=F32)

        def stage(j, s_cur, s_next, p_cur, p_prev, st, first=False, last=False):
            m, l, acc, alpha_prev = st
            if not last:
                s_next[...] = scores(j + 1)
            sc = s_cur[...]
            m_new = jnp.maximum(m, jnp.max(sc, axis=0, keepdims=True))
            alpha = jnp.exp2(m - m_new)
            p = jnp.exp2(sc - m_new)
            l = alpha * l + jnp.sum(p, axis=0, keepdims=True)
            if not first:
                acc = alpha_prev * acc + jnp.dot(vt_ref[j - 1], p_prev[...], preferred_element_type=F32)
            p_cur[...] = p.astype(BF16)
            return m_new, l, acc, alpha

        def pair(jj, st):
            st = stage(2 * jj + 1, s_b, s_a, p_b, p_a, st)
            return stage(2 * jj + 2, s_a, s_b, p_a, p_b, st)

        s_a[...] = scores(0)
        st = (jnp.full((1, tq), -1e30, F32), jnp.zeros((1, tq), F32),
              jnp.zeros((V_HEAD, tq), F32), jnp.ones((1, tq), F32))
        st = stage(0, s_a, s_b, p_a, p_b, st, first=True)
        st = lax.fori_loop(0, (nk - 2) // 2, pair, st)
        _, l, acc, alpha_last = stage(nk - 1, s_b, s_a, p_b, p_a, st, last=True)
        acc = alpha_last * acc + jnp.dot(vt_ref[nk - 1], p_b[...], preferred_element_type=F32)
        o_ref[pl.ds(q0, tq), :] = (acc / l).T.astype(BF16)
        return carry

    lax.fori_loop(0, nq, q_body, 0)


def attention(q, k, vt, tq):
    batch, nh, seq, _ = q.shape
    nt, tk = vt.shape[2], vt.shape[4]
    assert nt % 2 == 0
    return pl.pallas_call(
        functools.partial(_attn_kernel, tq=tq, tk=tk),
        grid=(batch, nh),
        in_specs=[pl.BlockSpec((None, None, seq, HEAD_PAD), lambda b, h: (b, h, 0, 0)),
                  pl.BlockSpec((None, None, seq, HEAD_PAD), lambda b, h: (b, h, 0, 0)),
                  pl.BlockSpec((None, None, nt, V_HEAD, tk), lambda b, h: (b, h, 0, 0, 0))],
        out_specs=pl.BlockSpec((None, seq, V_HEAD), lambda b, h: (b, 0, h)),
        out_shape=jax.ShapeDtypeStruct((batch, seq, nh * V_HEAD), BF16),
        scratch_shapes=[pltpu.VMEM((tk, tq), F32), pltpu.VMEM((tk, tq), F32),
                        pltpu.VMEM((tk, tq), BF16), pltpu.VMEM((tk, tq), BF16)],
        compiler_params=_cparams(("parallel", "parallel")),
        name="mla_attention",
    )(q, k, vt)


def _split_bf16(x):
    hi = x.astype(BF16)
    lo = (x - hi.astype(F32)).astype(BF16)
    return hi, lo


def _route(x, w_t, bias, top_k):
    nt = (((1,), (1,)), ((), ()))
    x_hi, x_lo = _split_bf16(x)
    w_hi, w_lo = _split_bf16(w_t)
    logits = (lax.dot_general(w_hi, x_hi, nt, preferred_element_type=F32)
              + lax.dot_general(w_hi, x_lo, nt, preferred_element_type=F32)
              + lax.dot_general(w_lo, x_hi, nt, preferred_element_type=F32)
              + bias)
    ne = logits.shape[0]
    eid = lax.broadcasted_iota(jnp.int32, logits.shape, 0)
    vals, idxs = [], []
    cur = logits
    for _ in range(top_k):
        mx = jnp.max(cur, axis=0, keepdims=True)
        ix = jnp.min(jnp.where(cur == mx, eid, ne), axis=0, keepdims=True)
        vals.append(mx)
        idxs.append(ix)
        cur = jnp.where(eid == ix, -jnp.inf, cur)
    ex = [jnp.exp(v - vals[0]) for v in vals]
    tot = ex[0]
    for e in ex[1:]:
        tot = tot + e
    return idxs, [e / tot for e in ex]


def _proj_ln_kernel(a_ref, w_ref, bias_ref, h_ref, g_ref, b_ref, wr_ref, br_ref,
                    o_ref, hp_ref, idx_ref, gate_ref):
    m = jnp.dot(a_ref[...], w_ref[...], preferred_element_type=F32) + bias_ref[...]
    o = _layer_norm(DEEPNORM_ALPHA * h_ref[...] + m, g_ref[...], b_ref[...])
    o_ref[...] = o
    _store_token_slabs(hp_ref, 0, _pack_pairs(o))
    idxs, gates = _route(o, wr_ref[...], br_ref[...], TOP_K)
    for kk in range(TOP_K):
        idx_ref[kk:kk + 1, :] = idxs[kk]
        gate_ref[kk:kk + 1, :] = gates[kk]


def proj_residual_ln(a, w, bias, h, g, b, wr_t, br, tm=512):
    n, d = h.shape
    kdim = a.shape[1]
    ne = wr_t.shape[0]
    nb = d // 256
    vec = pl.BlockSpec((1, d), lambda i: (0, 0))
    return pl.pallas_call(
        _proj_ln_kernel,
        grid=(n // tm,),
        in_specs=[pl.BlockSpec((tm, kdim), lambda i: (i, 0)),
                  pl.BlockSpec((kdim, d), lambda i: (0, 0)),
                  vec,
                  pl.BlockSpec((tm, d), lambda i: (i, 0)),
                  vec, vec,
                  pl.BlockSpec((ne, d), lambda i: (0, 0)),
                  pl.BlockSpec((ne, 1), lambda i: (0, 0))],
        out_specs=[pl.BlockSpec((tm, d), lambda i: (i, 0)),
                   pl.BlockSpec((tm * nb, 128), lambda i: (i, 0)),
                   pl.BlockSpec((TOP_K, tm), lambda i: (0, i)),
                   pl.BlockSpec((TOP_K, tm), lambda i: (0, i))],
        out_shape=[jax.ShapeDtypeStruct((n, d), F32),
                   jax.ShapeDtypeStruct((n * nb, 128), jnp.uint32),
                   jax.ShapeDtypeStruct((TOP_K, n), jnp.int32),
                   jax.ShapeDtypeStruct((TOP_K, n), F32)],
        compiler_params=_cparams(("parallel",)),
        name="proj_residual_ln",
    )(a, w, bias.reshape(1, d), h, g.reshape(1, d), b.reshape(1, d), wr_t, br.reshape(ne, 1))


def _chan_dft_kernel(h_ref, t_ref, z_ref, *, groups, gd):
    t = t_ref[...]
    for g in range(groups):
        sl = slice(g * gd, (g + 1) * gd)
        xg = h_ref[:, sl].astype(BF16)
        zz = jnp.dot(xg, t, preferred_element_type=F32)
        z_ref[0, :, sl] = zz[:, :gd].astype(BF16)
        z_ref[1, :, sl] = zz[:, gd:].astype(BF16)


def channel_dft(h, tab, tm):
    n, d = h.shape
    gd = d // FNET_GROUPS
    return pl.pallas_call(
        functools.partial(_chan_dft_kernel, groups=FNET_GROUPS, gd=gd),
        grid=(n // tm,),
        in_specs=[pl.BlockSpec((tm, d), lambda i: (i, 0)),
                  pl.BlockSpec((gd, 2 * gd), lambda i: (0, 0))],
        out_specs=pl.BlockSpec((None, 2, tm, d), lambda i: (i, 0, 0, 0)),
        out_shape=jax.ShapeDtypeStruct((n // tm, 2, tm, d), BF16),
        compiler_params=_cparams(("parallel",)),
        name="fnet_channel_dft",
    )(h, tab)


def _pos_dft_kernel(t_ref, z_ref, o_ref, acc_ref, *, norm):
    k = pl.program_id(2)

    @pl.when(k == 0)
    def _():
        acc_ref[...] = jnp.zeros_like(acc_ref)

    acc_ref[...] += jnp.dot(t_ref[...], z_ref[...], preferred_element_type=F32)

    @pl.when(k == pl.num_programs(2) - 1)
    def _():
        o_ref[...] = (acc_ref[...] * norm).astype(BF16)


def position_dft(tab, z, norm, tm, tk):
    batch, k2, d = z.shape
    seq = tab.shape[0]
    return pl.pallas_call(
        functools.partial(_pos_dft_kernel, norm=norm),
        grid=(batch, seq // tm, k2 // tk),
        in_specs=[pl.BlockSpec((tm, tk), lambda b, i, k: (i, k)),
                  pl.BlockSpec((None, tk, d), lambda b, i, k: (b, k, 0))],
        out_specs=pl.BlockSpec((None, tm, d), lambda b, i, k: (b, i, 0)),
        out_shape=jax.ShapeDtypeStruct((batch, seq, d), BF16),
        scratch_shapes=[pltpu.VMEM((tm, d), F32)],
        compiler_params=_cparams(("parallel", "parallel", "arbitrary")),
        name="fnet_position_dft",
    )(tab, z)


def _swiglu(hg, hu):
    g = jnp.minimum(hg, SWIGLU_LIMIT)
    l = jnp.clip(hu, -SWIGLU_LIMIT, SWIGLU_LIMIT)
    return g * (1.0 / (1.0 + jnp.exp(-SWIGLU_ALPHA * g))) * (l + 1.0)


def _expert_kernel(ue_ref, ustart_ref, ulen_ref, route_ref,
                   hp_hbm, wg_ref, bg_ref, wu_ref, bu_ref, wd_ref, bd_ref,
                   ys_hbm,
                   stage, xb, wgb, wub, wdb, act, ybuf, sem_g, sem_s,
                   *, rows, blocks, nf, tf, n_tok, nb):
    u = pl.program_id(0)
    s = pl.program_id(1)
    ulen = ulen_ref[u]
    in_share = rows // (nf + 1)
    out_share = rows // nf
    half = xb.shape[1] // 2
    n_assign = TOP_K * n_tok

    def row_in(base, i):
        tok = route_ref[base + i] & (ROUTE_TOK - 1)
        return pltpu.make_async_copy(hp_hbm.at[pl.ds(pl.multiple_of(tok * nb, nb), nb), :],
                                     stage.at[pl.ds(pl.multiple_of(i * nb, nb), nb), :], sem_g)

    def row_out(base, i):
        dest = lax.shift_right_logical(route_ref[base + i], ROUTE_SHIFT)
        return pltpu.make_async_copy(ybuf.at[pl.ds(pl.multiple_of(i * nb, nb), nb), :],
                                     ys_hbm.at[pl.ds(pl.multiple_of(dest * nb, nb), nb), :], sem_s)

    def loop_rows(fn):
        def single(i, c):
            fn(i)
            return c
        lax.fori_loop(0, rows, single, 0)

    def wait_rows(make):
        for i in range(rows):
            make(i).wait()

    base_next = ustart_ref[u + 1]
    base_prev = jnp.where(u > 0, ustart_ref[jnp.maximum(u - 1, 0)], n_assign)

    @pl.when(jnp.logical_and(u == 0, s == 0))
    def _first_step():
        ybuf[...] = jnp.zeros_like(ybuf)
        base = ustart_ref[0]
        loop_rows(lambda i: row_in(base, i).start())

    @pl.when(jnp.logical_and(s == 0, ulen > 0))
    def _await_rows():
        base = ustart_ref[u]
        wait_rows(lambda i: row_in(base, i))
        for start, size in blocks:
            @pl.when(start < ulen)
            def _():
                r = pl.ds(start, size)
                for t in range(nb):
                    lo, hi = _unpack_pairs(_load_token_slab_block(stage, start, size, nb, t))
                    xb[r, t * 128:(t + 1) * 128] = lo.astype(BF16)
                    xb[r, half + t * 128:half + (t + 1) * 128] = hi.astype(BF16)

    @pl.when(jnp.logical_and(s < nf, ulen > 0))
    def _gate_up():
        wgb[...] = wg_ref[...].astype(BF16)
        wub[...] = wu_ref[...].astype(BF16)
        for start, size in blocks:
            @pl.when(start < ulen)
            def _():
                if start == 0:
                    for c in range(in_share):
                        row_in(base_next, s * in_share + c).start()
                    for c in range(out_share):
                        row_out(base_prev, s * out_share + c).start()
                r = pl.ds(start, size)
                x = xb[r, :]
                hg = jnp.dot(x, wgb[...], preferred_element_type=F32) + bg_ref[...]
                hu = jnp.dot(x, wub[...], preferred_element_type=F32) + bu_ref[...]
                act[s, r, :] = _swiglu(hg, hu).astype(BF16)

    @pl.when(jnp.logical_and(s == nf, ulen > 0))
    def _down():
        wdb[...] = wd_ref[...].astype(BF16)
        wait_rows(lambda i: row_out(base_prev, i))
        for start, size in blocks:
            @pl.when(start < ulen)
            def _():
                if start == 0:
                    for c in range(nf * in_share, rows):
                        row_in(base_next, c).start()
                r = pl.ds(start, size)
                y = bd_ref[...] + jnp.dot(act[0, r, :], wdb[0:tf, :], preferred_element_type=F32)
                for f in range(1, nf):
                    y = y + jnp.dot(act[f, r, :], wdb[f * tf:(f + 1) * tf, :], preferred_element_type=F32)
                _store_token_slabs(ybuf, start, _pack_pairs(y))

        @pl.when(ulen_ref[u + 1] == 0)
        def _():
            own = ustart_ref[u]
            loop_rows(lambda i: row_out(own, i).start())
            loop_rows(lambda i: row_out(own, i).wait())
            loop_rows(lambda i: row_in(base_next, i).wait())


def moe_experts(hp, layer, ue, ustart, ulen, route, wg, bg, wu, bu, wd, bd, rows, tf):
    nl, ne, d, fdim = wg.shape
    nb = d // 256
    n_tok = hp.shape[0] // nb
    nf = fdim // tf
    n_units = ue.shape[0] - 1
    last_f = nf - 1
    blocks = tuple((st, min(sz, rows - st)) for st, sz in MOE_BLOCKS if st < rows)
    assert rows % nf == 0 and rows % (nf + 1) == 0 and sum(sz for _, sz in blocks) == rows
    assert n_tok <= ROUTE_TOK

    def f_idx(u, s, ulen_ref):
        return jnp.where(ulen_ref[u] > 0, jnp.minimum(s, last_f), last_f)

    w_in = pl.BlockSpec((None, None, d, tf),
                        lambda u, s, ue, us, ul, od: (layer, ue[u], 0, f_idx(u, s, ul)))
    b_in = pl.BlockSpec((None, None, 1, tf),
                        lambda u, s, ue, us, ul, od: (layer, ue[u], 0, f_idx(u, s, ul)))
    w_dn = pl.BlockSpec((None, None, fdim, d), lambda u, s, ue, us, ul, od: (layer, ue[u], 0, 0))
    b_dn = pl.BlockSpec((None, None, 1, d), lambda u, s, ue, us, ul, od: (layer, ue[u], 0, 0))
    any_spec = pl.BlockSpec(memory_space=pl.ANY)

    grid_spec = pltpu.PrefetchScalarGridSpec(
        num_scalar_prefetch=4,
        grid=(n_units, nf + 1),
        in_specs=[any_spec, w_in, b_in, w_in, b_in, w_dn, b_dn],
        out_specs=any_spec,
        scratch_shapes=[
            pltpu.VMEM((rows * nb, 128), jnp.uint32),
            pltpu.VMEM((rows, d), BF16),
            pltpu.VMEM((d, tf), BF16),
            pltpu.VMEM((d, tf), BF16),
            pltpu.VMEM((fdim, d), BF16),
            pltpu.VMEM((nf, rows, tf), BF16),
            pltpu.VMEM((rows * nb, 128), jnp.uint32),
            pltpu.SemaphoreType.DMA(()),
            pltpu.SemaphoreType.DMA(()),
        ],
    )
    return pl.pallas_call(
        functools.partial(_expert_kernel, rows=rows, blocks=blocks, nf=nf, tf=tf, n_tok=n_tok, nb=nb),
        grid_spec=grid_spec,
        out_shape=jax.ShapeDtypeStruct(((n_tok * TOP_K + rows) * nb, 128), jnp.uint32),
        compiler_params=_cparams(("arbitrary", "arbitrary")),
        name="moe_experts",
    )(ue, ustart, ulen, route, hp, wg, bg.reshape(nl, ne, 1, fdim), wu, bu.reshape(nl, ne, 1, fdim),
      wd, bd.reshape(nl, ne, 1, d))


def routing_tables(idx_t, rows):
    k, n = idx_t.shape
    n_assign = k * n
    flat_e = idx_t.T.reshape(-1)
    order = jnp.argsort(flat_e, stable=True).astype(jnp.int32)
    dest = (order % k) * n + order // k
    spare = n_assign + jnp.arange(rows, dtype=jnp.int32)
    route = jnp.concatenate([dest * ROUTE_TOK + order // k, spare * ROUTE_TOK])
    eids = jnp.arange(N_EXPERTS, dtype=jnp.int32)
    counts = jnp.sum(flat_e[:, None] == eids[None, :], axis=0, dtype=jnp.int32)
    cstart = jnp.cumsum(counts) - counts
    nu = (counts + rows - 1) // rows
    ucum = jnp.cumsum(nu)
    per = (counts + jnp.maximum(nu, 1) - 1) // jnp.maximum(nu, 1)
    per = ((per + 7) // 8) * 8
    n_units = n_assign // rows + N_EXPERTS
    uid = jnp.arange(n_units + 1, dtype=jnp.int32)
    valid = uid < ucum[-1]
    e_last = jnp.sum(ucum < ucum[-1], dtype=jnp.int32)
    e_u = jnp.where(valid, jnp.sum(uid[:, None] >= ucum[None, :], axis=1, dtype=jnp.int32), e_last)
    onehot = (e_u[:, None] == eids[None, :]).astype(jnp.int32)

    def of_unit(v):
        return jnp.sum(onehot * v[None, :], axis=1)

    j = uid - of_unit(ucum - nu)
    per_u = of_unit(per)
    ustart = jnp.where(valid, of_unit(cstart) + j * per_u, 0)
    ulen = jnp.where(valid, jnp.clip(of_unit(counts) - j * per_u, 0, per_u), 0)
    return e_u, ustart, ulen, route


def _combine_ln_kernel(y0, y1, y2, y3, gt_ref, h_ref, g_ref, b_ref, o_ref, f_ref, *, nb):
    tm, d = h_ref.shape
    half = d // 2
    gt = gt_ref[...]
    slots = (y0, y1, y2, y3)
    for t in range(nb):
        lo_sum = hi_sum = None
        for kk, y_ref in enumerate(slots):
            lo, hi = _unpack_pairs(_load_token_slab_block(y_ref, 0, tm, nb, t))
            gk = gt[:, kk:kk + 1]
            lo_sum = lo * gk if lo_sum is None else lo_sum + lo * gk
            hi_sum = hi * gk if hi_sum is None else hi_sum + hi * gk
        f_ref[:, t * 128:(t + 1) * 128] = lo_sum
        f_ref[:, half + t * 128:half + (t + 1) * 128] = hi_sum
    o_ref[...] = _layer_norm(DEEPNORM_ALPHA * h_ref[...] + f_ref[...], g_ref[...], b_ref[...])


def combine_residual_ln(ys, gates, h, g, b, tm=256):
    n, d = h.shape
    nb = d // 256
    nblk = n // tm
    vec = pl.BlockSpec((1, d), lambda i: (0, 0))

    def slot(kk):
        return pl.BlockSpec((tm * nb, 128), lambda i: (kk * nblk + i, 0))

    return pl.pallas_call(
        functools.partial(_combine_ln_kernel, nb=nb),
        grid=(nblk,),
        in_specs=[slot(0), slot(1), slot(2), slot(3),
                  pl.BlockSpec((tm, TOP_K), lambda i: (i, 0)),
                  pl.BlockSpec((tm, d), lambda i: (i, 0)), vec, vec],
        out_specs=pl.BlockSpec((tm, d), lambda i: (i, 0)),
        out_shape=jax.ShapeDtypeStruct((n, d), F32),
        scratch_shapes=[pltpu.VMEM((tm, d), F32)],
        compiler_params=_cparams(("parallel",)),
        name="moe_combine_ln",
    )(ys, ys, ys, ys, gates, h, g.reshape(1, d), b.reshape(1, d))


def _prep_mla(w_dq, q_norm, w_uq, w_dkv, kv_norm, w_ukv, w_o):
    nh = N_HEADS
    d = w_dq.shape[0]
    wuq = w_uq.reshape(Q_LORA, nh, QK_NOPE + QK_ROPE)
    pe = wuq[:, :, QK_NOPE:]
    z = jnp.zeros((Q_LORA, nh, QK_ROPE // 2), F32)
    wuq_p = jnp.concatenate([pe[:, :, 0::2], z, pe[:, :, 1::2], z], axis=-1)
    kpe = w_dkv[:, KV_LORA:]
    zk = jnp.zeros((d, QK_ROPE // 2), F32)
    wdkv = jnp.concatenate([w_dkv[:, :KV_LORA], kpe[:, 0::2], zk, kpe[:, 1::2], zk], axis=1)
    wukv = w_ukv.reshape(KV_LORA, nh, QK_NOPE + V_HEAD)
    return {
        "wdq": w_dq.astype(BF16),
        "qn": q_norm.reshape(1, Q_LORA),
        "wuq_n": wuq[:, :, :QK_NOPE].reshape(Q_LORA, nh * 128).astype(BF16),
        "wuq_p": wuq_p.reshape(Q_LORA, nh * 128).astype(BF16),
        "wdkv": wdkv.astype(BF16),
        "kvn": kv_norm.reshape(1, KV_LORA),
        "wuk": wukv[:, :, :QK_NOPE].reshape(KV_LORA, nh * 128).astype(BF16),
        "wuvt": wukv[:, :, QK_NOPE:].reshape(KV_LORA, nh * V_HEAD).T.astype(BF16),
        "wo": w_o.astype(BF16),
    }


def _rope_tables(positions):
    inv_freq = 1.0 / (ROPE_THETA ** (jnp.arange(0, QK_ROPE, 2, dtype=F32) / QK_ROPE))
    ang = positions.astype(F32).reshape(-1)[:, None] * inv_freq
    c, s = jnp.cos(ang), jnp.sin(ang)
    z = jnp.zeros_like(c)
    return jnp.concatenate([c, z, c, z], axis=1), jnp.concatenate([-s, z, s, z], axis=1)


def _dft_tables(seq, gd, tk):
    def cs(n):
        i = jnp.arange(n, dtype=jnp.int32)
        ang = ((i[:, None] * i[None, :]) % n).astype(F32) * (2.0 * math.pi / n)
        return jnp.cos(ang), jnp.sin(ang)

    cc, sc = cs(gd)
    chan = jnp.concatenate([cc, sc], axis=1).astype(BF16)

    a = 1 << ((seq.bit_length() - 1) // 2)
    col = jnp.arange(2 * seq, dtype=jnp.int32)
    within = col % (2 * tk)
    key = (col // (2 * tk)) * tk + within % tk
    quarter = (within >= tk).astype(jnp.int32) * (seq // 4)
    j1 = jnp.arange(seq // a, dtype=jnp.int32)[:, None]
    j0 = jnp.arange(a, dtype=jnp.int32)[:, None]
    ang_a = ((a * j1 * key[None, :]) % seq).astype(F32) * (2.0 * math.pi / seq)
    ang_b = ((j0 * key[None, :] + quarter[None, :]) % seq).astype(F32) * (2.0 * math.pi / seq)
    pos = (jnp.cos(ang_a)[:, None, :] * jnp.cos(ang_b)[None, :, :]
           - jnp.sin(ang_a)[:, None, :] * jnp.sin(ang_b)[None, :, :])
    return chan, pos.reshape(seq, 2 * seq).astype(BF16)


def kernel(x, positions, ln_in_g, ln_in_b, mla_w_dq, mla_q_norm, mla_w_uq, mla_w_dkv, mla_kv_norm,
           mla_w_ukv, mla_w_o, fnet_w, fnet_b, ln_mix_g, ln_mix_b, router_w, router_b, exp_w_gate,
           exp_b_gate, exp_w_up, exp_b_up, exp_w_down, exp_b_down, ln_moe_g, ln_moe_b):
    batch, seq, d = x.shape
    n = batch * seq
    t_seq = 512
    moe_tf = 256

    rope_c, rope_s = _rope_tables(positions)
    gd = d // FNET_GROUPS
    chan_tab, pos_tab = _dft_tables(seq, gd, t_seq)
    dft_norm = 1.0 / math.sqrt(seq * gd)
    zero_bias = jnp.zeros((d,), F32)

    h = layer_norm_rows(x.reshape(n, d), ln_in_g, ln_in_b)
    for i in range(DEPTH):
        j = i // 2
        if i % 2 == 0:
            w = _prep_mla(mla_w_dq[j], mla_q_norm[j], mla_w_uq[j], mla_w_dkv[j], mla_kv_norm[j],
                          mla_w_ukv[j], mla_w_o[j])
            q, k, vt = mla_projections(h, rope_c, rope_s, w, batch, seq, t_seq)
            o = attention(q, k, vt, tq=min(seq, 4096))
            h, hp, idx_t, gate_t = proj_residual_ln(
                o.reshape(n, N_HEADS * V_HEAD), w["wo"], zero_bias, h, ln_mix_g[i], ln_mix_b[i],
                router_w[i].T, router_b[i])
        else:
            z = channel_dft(h, chan_tab, t_seq).reshape(batch, 2 * seq, d)
            mixed = position_dft(pos_tab, z, dft_norm, tm=1024, tk=2 * t_seq)
            h, hp, idx_t, gate_t = proj_residual_ln(
                mixed.reshape(n, d), fnet_w[j].astype(BF16), fnet_b[j], h, ln_mix_g[i], ln_mix_b[i],
                router_w[i].T, router_b[i])
        ue, ustart, ulen, route = routing_tables(idx_t, MOE_ROWS)
        ys = moe_experts(hp, i, ue, ustart, ulen, route, exp_w_gate, exp_b_gate, exp_w_up,
                         exp_b_up, exp_w_down, exp_b_down, MOE_ROWS, moe_tf)
        h = combine_residual_ln(ys, gate_t.T, h, ln_moe_g[i], ln_moe_b[i])
    return h.reshape(batch, seq, d)
```

```python
import functools
import math

import jax
import jax.numpy as jnp
from jax import lax
from jax.experimental import pallas as pl
from jax.experimental.pallas import tpu as pltpu

F32 = jnp.float32
BF16 = jnp.bfloat16

DEPTH = 4
N_HEADS = 16
Q_LORA = 512
KV_LORA = 512
QK_NOPE = 128
QK_ROPE = 64
V_HEAD = 128
ROPE_THETA = 10000.0
FNET_GROUPS = 4
N_EXPERTS = 32
TOP_K = 4
SWIGLU_LIMIT = 7.0
SWIGLU_ALPHA = 1.702
DEEPNORM_ALPHA = (2 * DEPTH) ** 0.25
LN_EPS = 1e-5
RMS_EPS = 1e-6

HEAD_PAD = 256
VT_ROWS = V_HEAD + 16
VMEM_LIMIT = 56 * 1024 * 1024
MOE_ROWS = 1280
MOE_BLOCKS = ((0, 768), (768, 256), (1024, 128), (1152, 128))
ROUTE_SHIFT = 13
ROUTE_TOK = 1 << ROUTE_SHIFT


def _cparams(sem):
    return pltpu.CompilerParams(dimension_semantics=sem, vmem_limit_bytes=VMEM_LIMIT)


def _layer_norm(y, g, b):
    mu = jnp.mean(y, axis=-1, keepdims=True)
    yc = y - mu
    var = jnp.mean(yc * yc, axis=-1, keepdims=True)
    return yc * lax.rsqrt(var + LN_EPS) * g + b


def _rms_norm(y, g):
    ms = jnp.mean(y * y, axis=-1, keepdims=True)
    return y * lax.rsqrt(ms + RMS_EPS) * g


def _pack_pairs(y):
    w = y.shape[1] // 2
    lo = lax.bitcast_convert_type(y[:, :w].astype(BF16).astype(F32), jnp.uint32)
    hi = lax.bitcast_convert_type(y[:, w:].astype(BF16).astype(F32), jnp.uint32)
    return lax.shift_right_logical(lo, jnp.uint32(16)) | (hi & jnp.uint32(0xFFFF0000))


def _unpack_pairs(p):
    lo = lax.bitcast_convert_type(lax.shift_left(p, jnp.uint32(16)), F32)
    hi = lax.bitcast_convert_type(p & jnp.uint32(0xFFFF0000), F32)
    return lo, hi


def _store_token_slabs(ref, row0, packed):
    rows, w = packed.shape
    nb = w // 128
    for t in range(nb):
        ref[pl.ds(row0 * nb + t, rows, stride=nb), :] = packed[:, t * 128:(t + 1) * 128]


def _load_token_slab_block(ref, row0, rows, nb, t):
    return ref[pl.ds(row0 * nb + t, rows, stride=nb), :]


def _ln_kernel(x_ref, g_ref, b_ref, o_ref):
    o_ref[...] = _layer_norm(x_ref[...], g_ref[...], b_ref[...])


def layer_norm_rows(x, g, b, tm=512):
    n, d = x.shape
    row = pl.BlockSpec((tm, d), lambda i: (i, 0))
    vec = pl.BlockSpec((1, d), lambda i: (0, 0))
    return pl.pallas_call(
        _ln_kernel,
        grid=(n // tm,),
        in_specs=[row, vec, vec],
        out_specs=row,
        out_shape=jax.ShapeDtypeStruct((n, d), F32),
        compiler_params=_cparams(("parallel",)),
        name="ln_in",
    )(x, g.reshape(1, d), b.reshape(1, d))


def _rope(x, c, s):
    return x * c + pltpu.roll(x, 64, 1) * s


def _q_proj_kernel(h_ref, c_ref, s_ref, wdq_ref, qn_ref, wn_ref, wp_ref, q_ref, *, n_heads, q_scale):
    x = h_ref[...].astype(BF16)
    qa = jnp.dot(x, wdq_ref[...], preferred_element_type=F32)
    cq = _rms_norm(qa, qn_ref[...]).astype(BF16)
    qn = jnp.dot(cq, wn_ref[...], preferred_element_type=F32) * q_scale
    qp = jnp.dot(cq, wp_ref[...], preferred_element_type=F32) * q_scale
    c = c_ref[...]
    s = s_ref[...]
    for hd in range(n_heads):
        sl = slice(hd * 128, (hd + 1) * 128)
        q_ref[hd, :, 0:128] = qn[:, sl].astype(BF16)
        q_ref[hd, :, 128:256] = _rope(qp[:, sl], c, s).astype(BF16)


def _kv_proj_kernel(h_ref, c_ref, s_ref, wdkv_ref, kvn_ref, wk_ref, wvt_ref, k_ref, vt_ref, *, n_heads, kv_lora):
    x = h_ref[...].astype(BF16)
    kva = jnp.dot(x, wdkv_ref[...], preferred_element_type=F32)
    ckv = _rms_norm(kva[:, :kv_lora], kvn_ref[...]).astype(BF16)
    kpe = _rope(kva[:, kv_lora:], c_ref[...], s_ref[...]).astype(BF16)
    kn = jnp.dot(ckv, wk_ref[...], preferred_element_type=F32)
    vt = lax.dot_general(wvt_ref[...], ckv, (((1,), (1,)), ((), ())),
                         preferred_element_type=F32)
    tm = x.shape[0]
    extra = lax.broadcasted_iota(jnp.int32, (VT_ROWS - V_HEAD, tm), 0)
    ones_row = jnp.where(extra == 0, 1.0, 0.0).astype(BF16)
    for hd in range(n_heads):
        sl = slice(hd * 128, (hd + 1) * 128)
        k_ref[hd, :, 0:128] = kn[:, sl].astype(BF16)
        k_ref[hd, :, 128:256] = kpe
        vt_ref[hd, 0:V_HEAD, :] = vt[sl, :].astype(BF16)
        vt_ref[hd, V_HEAD:VT_ROWS, :] = ones_row


def mla_projections(h, rope_c, rope_s, w, batch, seq, tm):
    n, d = h.shape
    nh = N_HEADS
    nt = seq // tm
    q_scale = (QK_NOPE + QK_ROPE) ** -0.5 * math.log2(math.e)

    row = pl.BlockSpec((tm, d), lambda b, i: (b * nt + i, 0))
    tab = pl.BlockSpec((tm, 128), lambda b, i: (b * nt + i, 0))

    def full(a):
        return pl.BlockSpec(a.shape, lambda b, i: (0,) * a.ndim)

    head_out = pl.BlockSpec((None, nh, tm, HEAD_PAD), lambda b, i: (b, 0, i, 0))
    q = pl.pallas_call(
        functools.partial(_q_proj_kernel, n_heads=nh, q_scale=q_scale),
        grid=(batch, nt),
        in_specs=[row, tab, tab, full(w["wdq"]), full(w["qn"]), full(w["wuq_n"]), full(w["wuq_p"])],
        out_specs=head_out,
        out_shape=jax.ShapeDtypeStruct((batch, nh, seq, HEAD_PAD), BF16),
        compiler_params=_cparams(("parallel", "parallel")),
        name="mla_q_proj",
    )(h, rope_c, rope_s, w["wdq"], w["qn"], w["wuq_n"], w["wuq_p"])

    k, vt = pl.pallas_call(
        functools.partial(_kv_proj_kernel, n_heads=nh, kv_lora=KV_LORA),
        grid=(batch, nt),
        in_specs=[row, tab, tab, full(w["wdkv"]), full(w["kvn"]), full(w["wuk"]), full(w["wuvt"])],
        out_specs=[head_out,
                   pl.BlockSpec((None, nh, None, VT_ROWS, tm), lambda b, i: (b, 0, i, 0, 0))],
        out_shape=[jax.ShapeDtypeStruct((batch, nh, seq, HEAD_PAD), BF16),
                   jax.ShapeDtypeStruct((batch, nh, nt, VT_ROWS, tm), BF16)],
        compiler_params=_cparams(("parallel", "parallel")),
        name="mla_kv_proj",
    )(h, rope_c, rope_s, w["wdkv"], w["kvn"], w["wuk"], w["wuvt"])
    return q, k, vt


def _attn_kernel(q_ref, k_ref, vt_ref, o_ref, s_a, s_b, p_a, p_b, *, tq, tk):
    seq = q_ref.shape[0]
    nq = seq // tq
    nk = seq // tk
    nt = (((1,), (1,)), ((), ()))

    def q_body(qi, carry):
        q0 = pl.multiple_of(qi * tq, tq)
        q = q_ref[pl.ds(q0, tq), :]

        def scores_into(j, s_ref):
            k0 = pl.multiple_of(j * tk, tk)
            sc = lax.dot_general(k_ref[pl.ds(k0, tk), :], q, nt, preferred_element_type=F32)
            s_ref[...] = sc
            return jnp.max(sc, axis=0, keepdims=True)

        def stage(j, s_cur, s_next, p_cur, p_prev, st, first=False, last=False):
            m, mx_cur, acc, alpha_prev = st
            mx_next = mx_cur if last else scores_into(j + 1, s_next)
            m_new = jnp.maximum(m, mx_cur)
            alpha = jnp.exp2(m - m_new)
            if not first:
                acc = alpha_prev * acc + jnp.dot(vt_ref[j - 1], p_prev[...], preferred_element_type=F32)
            p_cur[...] = jnp.exp2(s_cur[...] - m_new).astype(BF16)
            return m_new, mx_next, acc, alpha

        def pair(jj, st):
            st = stage(2 * jj + 1, s_b, s_a, p_b, p_a, st)
            return stage(2 * jj + 2, s_a, s_b, p_a, p_b, st)

        mx0 = scores_into(0, s_a)
        st = (jnp.full((1, tq), -1e30, F32), mx0, jnp.zeros((VT_ROWS, tq), F32), jnp.ones((1, tq), F32))
        st = stage(0, s_a, s_b, p_a, p_b, st, first=True)
        st = lax.fori_loop(0, (nk - 2) // 2, pair, st)
        _, _, acc, alpha_last = stage(nk - 1, s_b, s_a, p_b, p_a, st, last=True)
        acc = alpha_last * acc + jnp.dot(vt_ref[nk - 1], p_b[...], preferred_element_type=F32)
        o = acc[0:V_HEAD, :] / acc[V_HEAD:V_HEAD + 1, :]
        o_ref[pl.ds(q0, tq), :] = o.T.astype(BF16)
        return carry

    lax.fori_loop(0, nq, q_body, 0)


def attention(q, k, vt, tq):
    batch, nh, seq, _ = q.shape
    nt, tk = vt.shape[2], vt.shape[4]
    assert nt % 2 == 0
    return pl.pallas_call(
        functools.partial(_attn_kernel, tq=tq, tk=tk),
        grid=(batch, nh),
        in_specs=[pl.BlockSpec((None, None, seq, HEAD_PAD), lambda b, h: (b, h, 0, 0)),
                  pl.BlockSpec((None, None, seq, HEAD_PAD), lambda b, h: (b, h, 0, 0)),
                  pl.BlockSpec((None, None, nt, VT_ROWS, tk), lambda b, h: (b, h, 0, 0, 0))],
        out_specs=pl.BlockSpec((None, seq, V_HEAD), lambda b, h: (b, 0, h)),
        out_shape=jax.ShapeDtypeStruct((batch, seq, nh * V_HEAD), BF16),
        scratch_shapes=[pltpu.VMEM((tk, tq), F32), pltpu.VMEM((tk, tq), F32),
                        pltpu.VMEM((tk, tq), BF16), pltpu.VMEM((tk, tq), BF16)],
        compiler_params=_cparams(("parallel", "parallel")),
        name="mla_attention",
    )(q, k, vt)


def _split_bf16(x):
    hi = x.astype(BF16)
    lo = (x - hi.astype(F32)).astype(BF16)
    return hi, lo


def _route(x, w_t, bias, top_k):
    nt = (((1,), (1,)), ((), ()))
    x_hi, x_lo = _split_bf16(x)
    w_hi, w_lo = _split_bf16(w_t)
    logits = (lax.dot_general(w_hi, x_hi, nt, preferred_element_type=F32)
              + lax.dot_general(w_hi, x_lo, nt, preferred_element_type=F32)
              + lax.dot_general(w_lo, x_hi, nt, preferred_element_type=F32)
              + bias)
    ne = logits.shape[0]
    eid = lax.broadcasted_iota(jnp.int32, logits.shape, 0)
    vals, idxs = [], []
    cur = logits
    for _ in range(top_k):
        mx = jnp.max(cur, axis=0, keepdims=True)
        ix = jnp.min(jnp.where(cur == mx, eid, ne), axis=0, keepdims=True)
        vals.append(mx)
        idxs.append(ix)
        cur = jnp.where(eid == ix, -jnp.inf, cur)
    ex = [jnp.exp(v - vals[0]) for v in vals]
    tot = ex[0]
    for e in ex[1:]:
        tot = tot + e
    return idxs, [e / tot for e in ex]


def _proj_ln_kernel(a_ref, w_ref, bias_ref, h_ref, g_ref, b_ref, wr_ref, br_ref,
                    o_ref, hp_ref, idx_ref, gate_ref):
    m = jnp.dot(a_ref[...], w_ref[...], preferred_element_type=F32) + bias_ref[...]
    o = _layer_norm(DEEPNORM_ALPHA * h_ref[...] + m, g_ref[...], b_ref[...])
    o_ref[...] = o
    _store_token_slabs(hp_ref, 0, _pack_pairs(o))
    idxs, gates = _route(o, wr_ref[...], br_ref[...], TOP_K)
    for kk in range(TOP_K):
        idx_ref[kk:kk + 1, :] = idxs[kk]
        gate_ref[kk:kk + 1, :] = gates[kk]


def proj_residual_ln(a, w, bias, h, g, b, wr_t, br, tm=512):
    n, d = h.shape
    kdim = a.shape[1]
    ne = wr_t.shape[0]
    nb = d // 256
    vec = pl.BlockSpec((1, d), lambda i: (0, 0))
    return pl.pallas_call(
        _proj_ln_kernel,
        grid=(n // tm,),
        in_specs=[pl.BlockSpec((tm, kdim), lambda i: (i, 0)),
                  pl.BlockSpec((kdim, d), lambda i: (0, 0)),
                  vec,
                  pl.BlockSpec((tm, d), lambda i: (i, 0)),
                  vec, vec,
                  pl.BlockSpec((ne, d), lambda i: (0, 0)),
                  pl.BlockSpec((ne, 1), lambda i: (0, 0))],
        out_specs=[pl.BlockSpec((tm, d), lambda i: (i, 0)),
                   pl.BlockSpec((tm * nb, 128), lambda i: (i, 0)),
                   pl.BlockSpec((TOP_K, tm), lambda i: (0, i)),
                   pl.BlockSpec((TOP_K, tm), lambda i: (0, i))],
        out_shape=[jax.ShapeDtypeStruct((n, d), F32),
                   jax.ShapeDtypeStruct((n * nb, 128), jnp.uint32),
                   jax.ShapeDtypeStruct((TOP_K, n), jnp.int32),
                   jax.ShapeDtypeStruct((TOP_K, n), F32)],
        compiler_params=_cparams(("parallel",)),
        name="proj_residual_ln",
    )(a, w, bias.reshape(1, d), h, g.reshape(1, d), b.reshape(1, d), wr_t, br.reshape(ne, 1))


def _chan_dft_kernel(h_ref, t_ref, z_ref, *, groups, gd):
    t = t_ref[...]
    for g in range(groups):
        sl = slice(g * gd, (g + 1) * gd)
        xg = h_ref[:, sl].astype(BF16)
        zz = jnp.dot(xg, t, preferred_element_type=F32)
        z_ref[0, :, sl] = zz[:, :gd].astype(BF16)
        z_ref[1, :, sl] = zz[:, gd:].astype(BF16)


def channel_dft(h, tab, tm):
    n, d = h.shape
    gd = d // FNET_GROUPS
    return pl.pallas_call(
        functools.partial(_chan_dft_kernel, groups=FNET_GROUPS, gd=gd),
        grid=(n // tm,),
        in_specs=[pl.BlockSpec((tm, d), lambda i: (i, 0)),
                  pl.BlockSpec((gd, 2 * gd), lambda i: (0, 0))],
        out_specs=pl.BlockSpec((None, 2, tm, d), lambda i: (i, 0, 0, 0)),
        out_shape=jax.ShapeDtypeStruct((n // tm, 2, tm, d), BF16),
        compiler_params=_cparams(("parallel",)),
        name="fnet_channel_dft",
    )(h, tab)


def _pos_dft_kernel(t_ref, z_ref, o_ref, acc_ref, *, norm):
    k = pl.program_id(2)

    @pl.when(k == 0)
    def _():
        acc_ref[...] = jnp.zeros_like(acc_ref)

    acc_ref[...] += jnp.dot(t_ref[...], z_ref[...], preferred_element_type=F32)

    @pl.when(k == pl.num_programs(2) - 1)
    def _():
        o_ref[...] = (acc_ref[...] * norm).astype(BF16)


def position_dft(tab, z, norm, tm, tk):
    batch, k2, d = z.shape
    seq = tab.shape[0]
    return pl.pallas_call(
        functools.partial(_pos_dft_kernel, norm=norm),
        grid=(batch, seq // tm, k2 // tk),
        in_specs=[pl.BlockSpec((tm, tk), lambda b, i, k: (i, k)),
                  pl.BlockSpec((None, tk, d), lambda b, i, k: (b, k, 0))],
        out_specs=pl.BlockSpec((None, tm, d), lambda b, i, k: (b, i, 0)),
        out_shape=jax.ShapeDtypeStruct((batch, seq, d), BF16),
        scratch_shapes=[pltpu.VMEM((tm, d), F32)],
        compiler_params=_cparams(("parallel", "parallel", "arbitrary")),
        name="fnet_position_dft",
    )(tab, z)


def _swiglu(hg, hu):
    g = jnp.minimum(hg, SWIGLU_LIMIT)
    l = jnp.clip(hu, -SWIGLU_LIMIT, SWIGLU_LIMIT)
    return g * (1.0 / (1.0 + jnp.exp(-SWIGLU_ALPHA * g))) * (l + 1.0)


def _expert_kernel(ue_ref, ustart_ref, ulen_ref, route_ref,
                   hp_hbm, wg_ref, bg_ref, wu_ref, bu_ref, wd_ref, bd_ref,
                   ys_hbm,
                   stage, xb, wgb, wub, wdb, act, ybuf, sem_g, sem_s,
                   *, rows, blocks, nf, tf, n_tok, nb):
    u = pl.program_id(0)
    s = pl.program_id(1)
    ulen = ulen_ref[u]
    in_share = rows // (nf + 1)
    out_share = rows // nf
    half = xb.shape[1] // 2
    n_assign = TOP_K * n_tok

    def row_in(base, i):
        tok = route_ref[base + i] & (ROUTE_TOK - 1)
        return pltpu.make_async_copy(hp_hbm.at[pl.ds(pl.multiple_of(tok * nb, nb), nb), :],
                                     stage.at[pl.ds(pl.multiple_of(i * nb, nb), nb), :], sem_g)

    def row_out(base, i):
        dest = lax.shift_right_logical(route_ref[base + i], ROUTE_SHIFT)
        return pltpu.make_async_copy(ybuf.at[pl.ds(pl.multiple_of(i * nb, nb), nb), :],
                                     ys_hbm.at[pl.ds(pl.multiple_of(dest * nb, nb), nb), :], sem_s)

    def loop_rows(fn):
        def single(i, c):
            fn(i)
            return c
        lax.fori_loop(0, rows, single, 0)

    def wait_rows(make):
        for i in range(rows):
            make(i).wait()

    base_next = ustart_ref[u + 1]
    base_prev = jnp.where(u > 0, ustart_ref[jnp.maximum(u - 1, 0)], n_assign)

    @pl.when(jnp.logical_and(u == 0, s == 0))
    def _first_step():
        ybuf[...] = jnp.zeros_like(ybuf)
        base = ustart_ref[0]
        loop_rows(lambda i: row_in(base, i).start())

    @pl.when(jnp.logical_and(s == 0, ulen > 0))
    def _await_rows():
        base = ustart_ref[u]
        wait_rows(lambda i: row_in(base, i))
        for start, size in blocks:
            @pl.when(start < ulen)
            def _():
                r = pl.ds(start, size)
                for t in range(nb):
                    lo, hi = _unpack_pairs(_load_token_slab_block(stage, start, size, nb, t))
                    xb[r, t * 128:(t + 1) * 128] = lo.astype(BF16)
                    xb[r, half + t * 128:half + (t + 1) * 128] = hi.astype(BF16)

    @pl.when(jnp.logical_and(s < nf, ulen > 0))
    def _gate_up():
        wgb[...] = wg_ref[...].astype(BF16)
        wub[...] = wu_ref[...].astype(BF16)
        for start, size in blocks:
            @pl.when(start < ulen)
            def _():
                if start == 0:
                    for c in range(in_share):
                        row_in(base_next, s * in_share + c).start()
                    for c in range(out_share):
                        row_out(base_prev, s * out_share + c).start()
                r = pl.ds(start, size)
                x = xb[r, :]
                hg = jnp.dot(x, wgb[...], preferred_element_type=F32) + bg_ref[...]
                hu = jnp.dot(x, wub[...], preferred_element_type=F32) + bu_ref[...]
                act[s, r, :] = _swiglu(hg, hu).astype(BF16)

    @pl.when(jnp.logical_and(s == nf, ulen > 0))
    def _down():
        wdb[...] = wd_ref[...].astype(BF16)
        wait_rows(lambda i: row_out(base_prev, i))
        for c in range(nf * in_share, rows):
            row_in(base_next, c).start()

        def row_block(start, size):
            r = pl.ds(start, size)
            y = bd_ref[...] + jnp.dot(act[0, r, :], wdb[0:tf, :], preferred_element_type=F32)
            for f in range(1, nf):
                y = y + jnp.dot(act[f, r, :], wdb[f * tf:(f + 1) * tf, :], preferred_element_type=F32)
            _store_token_slabs(ybuf, start, _pack_pairs(y))

        row_block(*blocks[0])
        for start, size in blocks[1:]:
            pl.when(start < ulen)(functools.partial(row_block, start, size))

        @pl.when(ulen_ref[u + 1] == 0)
        def _():
            own = ustart_ref[u]
            loop_rows(lambda i: row_out(own, i).start())
            loop_rows(lambda i: row_out(own, i).wait())
            loop_rows(lambda i: row_in(base_next, i).wait())


def moe_experts(hp, layer, ue, ustart, ulen, route, wg, bg, wu, bu, wd, bd, rows, tf):
    nl, ne, d, fdim = wg.shape
    nb = d // 256
    n_tok = hp.shape[0] // nb
    nf = fdim // tf
    n_units = ue.shape[0] - 1
    last_f = nf - 1
    blocks = tuple((st, min(sz, rows - st)) for st, sz in MOE_BLOCKS if st < rows)
    assert rows % nf == 0 and rows % (nf + 1) == 0 and sum(sz for _, sz in blocks) == rows
    assert n_tok <= ROUTE_TOK

    def f_idx(u, s, ulen_ref):
        return jnp.where(ulen_ref[u] > 0, jnp.minimum(s, last_f), last_f)

    w_in = pl.BlockSpec((None, None, d, tf),
                        lambda u, s, ue, us, ul, od: (layer, ue[u], 0, f_idx(u, s, ul)))
    b_in = pl.BlockSpec((None, None, 1, tf),
                        lambda u, s, ue, us, ul, od: (layer, ue[u], 0, f_idx(u, s, ul)))
    w_dn = pl.BlockSpec((None, None, fdim, d), lambda u, s, ue, us, ul, od: (layer, ue[u], 0, 0))
    b_dn = pl.BlockSpec((None, None, 1, d), lambda u, s, ue, us, ul, od: (layer, ue[u], 0, 0))
    any_spec = pl.BlockSpec(memory_space=pl.ANY)

    grid_spec = pltpu.PrefetchScalarGridSpec(
        num_scalar_prefetch=4,
        grid=(n_units, nf + 1),
        in_specs=[any_spec, w_in, b_in, w_in, b_in, w_dn, b_dn],
        out_specs=any_spec,
        scratch_shapes=[
            pltpu.VMEM((rows * nb, 128), jnp.uint32),
            pltpu.VMEM((rows, d), BF16),
            pltpu.VMEM((d, tf), BF16),
            pltpu.VMEM((d, tf), BF16),
            pltpu.VMEM((fdim, d), BF16),
            pltpu.VMEM((nf, rows, tf), BF16),
            pltpu.VMEM((rows * nb, 128), jnp.uint32),
            pltpu.SemaphoreType.DMA(()),
            pltpu.SemaphoreType.DMA(()),
        ],
    )
    return pl.pallas_call(
        functools.partial(_expert_kernel, rows=rows, blocks=blocks, nf=nf, tf=tf, n_tok=n_tok, nb=nb),
        grid_spec=grid_spec,
        out_shape=jax.ShapeDtypeStruct(((n_tok * TOP_K + rows) * nb, 128), jnp.uint32),
        compiler_params=_cparams(("arbitrary", "arbitrary")),
        name="moe_experts",
    )(ue, ustart, ulen, route, hp, wg, bg.reshape(nl, ne, 1, fdim), wu, bu.reshape(nl, ne, 1, fdim),
      wd, bd.reshape(nl, ne, 1, d))


def routing_tables(idx_t, rows):
    k, n = idx_t.shape
    n_assign = k * n
    flat_e = idx_t.T.reshape(-1)
    order = jnp.argsort(flat_e, stable=True).astype(jnp.int32)
    dest = (order % k) * n + order // k
    spare = n_assign + jnp.arange(rows, dtype=jnp.int32)
    route = jnp.concatenate([dest * ROUTE_TOK + order // k, spare * ROUTE_TOK])
    eids = jnp.arange(N_EXPERTS, dtype=jnp.int32)
    counts = jnp.sum(flat_e[:, None] == eids[None, :], axis=0, dtype=jnp.int32)
    cstart = jnp.cumsum(counts) - counts
    nu = (counts + rows - 1) // rows
    ucum = jnp.cumsum(nu)
    per = (counts + jnp.maximum(nu, 1) - 1) // jnp.maximum(nu, 1)
    per = ((per + 7) // 8) * 8
    n_units = n_assign // rows + N_EXPERTS
    uid = jnp.arange(n_units + 1, dtype=jnp.int32)
    valid = uid < ucum[-1]
    e_last = jnp.sum(ucum < ucum[-1], dtype=jnp.int32)
    e_u = jnp.where(valid, jnp.sum(uid[:, None] >= ucum[None, :], axis=1, dtype=jnp.int32), e_last)
    onehot = (e_u[:, None] == eids[None, :]).astype(jnp.int32)

    def of_unit(v):
        return jnp.sum(onehot * v[None, :], axis=1)

    j = uid - of_unit(ucum - nu)
    per_u = of_unit(per)
    ustart = jnp.where(valid, of_unit(cstart) + j * per_u, 0)
    ulen = jnp.where(valid, jnp.clip(of_unit(counts) - j * per_u, 0, per_u), 0)
    return e_u, ustart, ulen, route


def _combine_ln_kernel(y0, y1, y2, y3, gt_ref, h_ref, g_ref, b_ref, o_ref, f_ref, *, nb):
    tm, d = h_ref.shape
    half = d // 2
    gt = gt_ref[...]
    slots = (y0, y1, y2, y3)
    for t in range(nb):
        lo_sum = hi_sum = None
        for kk, y_ref in enumerate(slots):
            lo, hi = _unpack_pairs(_load_token_slab_block(y_ref, 0, tm, nb, t))
            gk = gt[:, kk:kk + 1]
            lo_sum = lo * gk if lo_sum is None else lo_sum + lo * gk
            hi_sum = hi * gk if hi_sum is None else hi_sum + hi * gk
        f_ref[:, t * 128:(t + 1) * 128] = lo_sum
        f_ref[:, half + t * 128:half + (t + 1) * 128] = hi_sum
    o_ref[...] = _layer_norm(DEEPNORM_ALPHA * h_ref[...] + f_ref[...], g_ref[...], b_ref[...])


def combine_residual_ln(ys, gates, h, g, b, tm=256):
    n, d = h.shape
    nb = d // 256
    nblk = n // tm
    vec = pl.BlockSpec((1, d), lambda i: (0, 0))

    def slot(kk):
        return pl.BlockSpec((tm * nb, 128), lambda i: (kk * nblk + i, 0))

    return pl.pallas_call(
        functools.partial(_combine_ln_kernel, nb=nb),
        grid=(nblk,),
        in_specs=[slot(0), slot(1), slot(2), slot(3),
                  pl.BlockSpec((tm, TOP_K), lambda i: (i, 0)),
                  pl.BlockSpec((tm, d), lambda i: (i, 0)), vec, vec],
        out_specs=pl.BlockSpec((tm, d), lambda i: (i, 0)),
        out_shape=jax.ShapeDtypeStruct((n, d), F32),
        scratch_shapes=[pltpu.VMEM((tm, d), F32)],
        compiler_params=_cparams(("parallel",)),
        name="moe_combine_ln",
    )(ys, ys, ys, ys, gates, h, g.reshape(1, d), b.reshape(1, d))


def _prep_mla(w_dq, q_norm, w_uq, w_dkv, kv_norm, w_ukv, w_o):
    nh = N_HEADS
    d = w_dq.shape[0]
    wuq = w_uq.reshape(Q_LORA, nh, QK_NOPE + QK_ROPE)
    pe = wuq[:, :, QK_NOPE:]
    z = jnp.zeros((Q_LORA, nh, QK_ROPE // 2), F32)
    wuq_p = jnp.concatenate([pe[:, :, 0::2], z, pe[:, :, 1::2], z], axis=-1)
    kpe = w_dkv[:, KV_LORA:]
    zk = jnp.zeros((d, QK_ROPE // 2), F32)
    wdkv = jnp.concatenate([w_dkv[:, :KV_LORA], kpe[:, 0::2], zk, kpe[:, 1::2], zk], axis=1)
    wukv = w_ukv.reshape(KV_LORA, nh, QK_NOPE + V_HEAD)
    return {
        "wdq": w_dq.astype(BF16),
        "qn": q_norm.reshape(1, Q_LORA),
        "wuq_n": wuq[:, :, :QK_NOPE].reshape(Q_LORA, nh * 128).astype(BF16),
        "wuq_p": wuq_p.reshape(Q_LORA, nh * 128).astype(BF16),
        "wdkv": wdkv.astype(BF16),
        "kvn": kv_norm.reshape(1, KV_LORA),
        "wuk": wukv[:, :, :QK_NOPE].reshape(KV_LORA, nh * 128).astype(BF16),
        "wuvt": wukv[:, :, QK_NOPE:].reshape(KV_LORA, nh * V_HEAD).T.astype(BF16),
        "wo": w_o.astype(BF16),
    }


def _rope_tables(positions):
    inv_freq = 1.0 / (ROPE_THETA ** (jnp.arange(0, QK_ROPE, 2, dtype=F32) / QK_ROPE))
    ang = positions.astype(F32).reshape(-1)[:, None] * inv_freq
    c, s = jnp.cos(ang), jnp.sin(ang)
    z = jnp.zeros_like(c)
    return jnp.concatenate([c, z, c, z], axis=1), jnp.concatenate([-s, z, s, z], axis=1)


def _dft_tables(seq, gd, tk):
    def cs(n):
        i = jnp.arange(n, dtype=jnp.int32)
        ang = ((i[:, None] * i[None, :]) % n).astype(F32) * (2.0 * math.pi / n)
        return jnp.cos(ang), jnp.sin(ang)

    cc, sc = cs(gd)
    chan = jnp.concatenate([cc, sc], axis=1).astype(BF16)

    a = 1 << ((seq.bit_length() - 1) // 2)
    col = jnp.arange(2 * seq, dtype=jnp.int32)
    within = col % (2 * tk)
    key = (col // (2 * tk)) * tk + within % tk
    quarter = (within >= tk).astype(jnp.int32) * (seq // 4)
    j1 = jnp.arange(seq // a, dtype=jnp.int32)[:, None]
    j0 = jnp.arange(a, dtype=jnp.int32)[:, None]
    ang_a = ((a * j1 * key[None, :]) % seq).astype(F32) * (2.0 * math.pi / seq)
    ang_b = ((j0 * key[None, :] + quarter[None, :]) % seq).astype(F32) * (2.0 * math.pi / seq)
    pos = (jnp.cos(ang_a)[:, None, :] * jnp.cos(ang_b)[None, :, :]
           - jnp.sin(ang_a)[:, None, :] * jnp.sin(ang_b)[None, :, :])
    return chan, pos.reshape(seq, 2 * seq).astype(BF16)


def kernel(x, positions, ln_in_g, ln_in_b, mla_w_dq, mla_q_norm, mla_w_uq, mla_w_dkv, mla_kv_norm,
           mla_w_ukv, mla_w_o, fnet_w, fnet_b, ln_mix_g, ln_mix_b, router_w, router_b, exp_w_gate,
           exp_b_gate, exp_w_up, exp_b_up, exp_w_down, exp_b_down, ln_moe_g, ln_moe_b):
    batch, seq, d = x.shape
    n = batch * seq
    t_seq = 512
    moe_tf = 256

    rope_c, rope_s = _rope_tables(positions)
    gd = d // FNET_GROUPS
    chan_tab, pos_tab = _dft_tables(seq, gd, t_seq)
    dft_norm = 1.0 / math.sqrt(seq * gd)
    zero_bias = jnp.zeros((d,), F32)

    h = layer_norm_rows(x.reshape(n, d), ln_in_g, ln_in_b)
    for i in range(DEPTH):
        j = i // 2
        if i % 2 == 0:
            w = _prep_mla(mla_w_dq[j], mla_q_norm[j], mla_w_uq[j], mla_w_dkv[j], mla_kv_norm[j],
                          mla_w_ukv[j], mla_w_o[j])
            q, k, vt = mla_projections(h, rope_c, rope_s, w, batch, seq, t_seq)
            o = attention(q, k, vt, tq=min(seq, 4096))
            h, hp, idx_t, gate_t = proj_residual_ln(
                o.reshape(n, N_HEADS * V_HEAD), w["wo"], zero_bias, h, ln_mix_g[i], ln_mix_b[i],
                router_w[i].T, router_b[i])
        else:
            z = channel_dft(h, chan_tab, t_seq).reshape(batch, 2 * seq, d)
            mixed = position_dft(pos_tab, z, dft_norm, tm=1024, tk=2 * t_seq)
            h, hp, idx_t, gate_t = proj_residual_ln(
                mixed.reshape(n, d), fnet_w[j].astype(BF16), fnet_b[j], h, ln_mix_g[i], ln_mix_b[i],
                router_w[i].T, router_b[i])
        ue, ustart, ulen, route = routing_tables(idx_t, MOE_ROWS)
        ys = moe_experts(hp, i, ue, ustart, ulen, route, exp_w_gate, exp_b_gate, exp_w_up,
                         exp_b_up, exp_w_down, exp_b_down, MOE_ROWS, moe_tf)
        h = combine_residual_ln(ys, gate_t.T, h, ln_moe_g[i], ln_moe_b[i])
    return h.reshape(batch, seq, d)
```

```python
import functools
import math

import jax
import jax.numpy as jnp
from jax import lax
from jax.experimental import pallas as pl
from jax.experimental.pallas import tpu as pltpu

F32 = jnp.float32
BF16 = jnp.bfloat16

DEPTH = 4
N_HEADS = 16
Q_LORA = 512
KV_LORA = 512
QK_NOPE = 128
QK_ROPE = 64
V_HEAD = 128
ROPE_THETA = 10000.0
FNET_GROUPS = 4
N_EXPERTS = 32
TOP_K = 4
SWIGLU_LIMIT = 7.0
SWIGLU_ALPHA = 1.702
DEEPNORM_ALPHA = (2 * DEPTH) ** 0.25
LN_EPS = 1e-5
RMS_EPS = 1e-6

HEAD_PAD = 256
VT_ROWS = V_HEAD + 16
VMEM_LIMIT = 56 * 1024 * 1024
MOE_ROWS = 1280
MOE_BLOCKS = ((0, 768), (768, 256), (1024, 128), (1152, 128))
MOE_UP_BLOCKS = ((0, 1024), (1024, 128), (1152, 128))
ROUTE_SHIFT = 13
ROUTE_TOK = 1 << ROUTE_SHIFT


def _cparams(sem):
    return pltpu.CompilerParams(dimension_semantics=sem, vmem_limit_bytes=VMEM_LIMIT)


def _layer_norm(y, g, b):
    mu = jnp.mean(y, axis=-1, keepdims=True)
    yc = y - mu
    var = jnp.mean(yc * yc, axis=-1, keepdims=True)
    return yc * lax.rsqrt(var + LN_EPS) * g + b


def _rms_norm(y, g):
    ms = jnp.mean(y * y, axis=-1, keepdims=True)
    return y * lax.rsqrt(ms + RMS_EPS) * g


def _pack_pairs(y):
    w = y.shape[1] // 2
    lo = lax.bitcast_convert_type(y[:, :w].astype(BF16).astype(F32), jnp.uint32)
    hi = lax.bitcast_convert_type(y[:, w:].astype(BF16).astype(F32), jnp.uint32)
    return lax.shift_right_logical(lo, jnp.uint32(16)) | (hi & jnp.uint32(0xFFFF0000))


def _unpack_pairs(p):
    lo = lax.bitcast_convert_type(lax.shift_left(p, jnp.uint32(16)), F32)
    hi = lax.bitcast_convert_type(p & jnp.uint32(0xFFFF0000), F32)
    return lo, hi


def _store_token_slabs(ref, row0, packed):
    rows, w = packed.shape
    nb = w // 128
    for t in range(nb):
        ref[pl.ds(row0 * nb + t, rows, stride=nb), :] = packed[:, t * 128:(t + 1) * 128]


def _load_token_slab_block(ref, row0, rows, nb, t):
    return ref[pl.ds(row0 * nb + t, rows, stride=nb), :]


def _ln_kernel(x_ref, g_ref, b_ref, o_ref):
    o_ref[...] = _layer_norm(x_ref[...], g_ref[...], b_ref[...])


def layer_norm_rows(x, g, b, tm=512):
    n, d = x.shape
    row = pl.BlockSpec((tm, d), lambda i: (i, 0))
    vec = pl.BlockSpec((1, d), lambda i: (0, 0))
    return pl.pallas_call(
        _ln_kernel,
        grid=(n // tm,),
        in_specs=[row, vec, vec],
        out_specs=row,
        out_shape=jax.ShapeDtypeStruct((n, d), F32),
        compiler_params=_cparams(("parallel",)),
        name="ln_in",
    )(x, g.reshape(1, d), b.reshape(1, d))


def _rope(x, c, s):
    return x * c + pltpu.roll(x, 64, 1) * s


def _q_proj_kernel(h_ref, c_ref, s_ref, wdq_ref, qn_ref, wn_ref, wp_ref, q_ref, *, n_heads, q_scale):
    x = h_ref[...].astype(BF16)
    qa = jnp.dot(x, wdq_ref[...], preferred_element_type=F32)
    cq = _rms_norm(qa, qn_ref[...]).astype(BF16)
    qn = jnp.dot(cq, wn_ref[...], preferred_element_type=F32) * q_scale
    qp = jnp.dot(cq, wp_ref[...], preferred_element_type=F32) * q_scale
    c = c_ref[...]
    s = s_ref[...]
    for hd in range(n_heads):
        sl = slice(hd * 128, (hd + 1) * 128)
        q_ref[hd, :, 0:128] = qn[:, sl].astype(BF16)
        q_ref[hd, :, 128:256] = _rope(qp[:, sl], c, s).astype(BF16)


def _kv_proj_kernel(h_ref, c_ref, s_ref, wdkv_ref, kvn_ref, wk_ref, wvt_ref, k_ref, vt_ref, *, n_heads, kv_lora):
    x = h_ref[...].astype(BF16)
    kva = jnp.dot(x, wdkv_ref[...], preferred_element_type=F32)
    ckv = _rms_norm(kva[:, :kv_lora], kvn_ref[...]).astype(BF16)
    kpe = _rope(kva[:, kv_lora:], c_ref[...], s_ref[...]).astype(BF16)
    kn = jnp.dot(ckv, wk_ref[...], preferred_element_type=F32)
    vt = lax.dot_general(wvt_ref[...], ckv, (((1,), (1,)), ((), ())),
                         preferred_element_type=F32)
    tm = x.shape[0]
    extra = lax.broadcasted_iota(jnp.int32, (VT_ROWS - V_HEAD, tm), 0)
    ones_row = jnp.where(extra == 0, 1.0, 0.0).astype(BF16)
    for hd in range(n_heads):
        sl = slice(hd * 128, (hd + 1) * 128)
        k_ref[hd, :, 0:128] = kn[:, sl].astype(BF16)
        k_ref[hd, :, 128:256] = kpe
        vt_ref[hd, 0:V_HEAD, :] = vt[sl, :].astype(BF16)
        vt_ref[hd, V_HEAD:VT_ROWS, :] = ones_row


def mla_projections(h, rope_c, rope_s, w, batch, seq, tm):
    n, d = h.shape
    nh = N_HEADS
    nt = seq // tm
    q_scale = (QK_NOPE + QK_ROPE) ** -0.5 * math.log2(math.e)

    row = pl.BlockSpec((tm, d), lambda b, i: (b * nt + i, 0))
    tab = pl.BlockSpec((tm, 128), lambda b, i: (b * nt + i, 0))

    def full(a):
        return pl.BlockSpec(a.shape, lambda b, i: (0,) * a.ndim)

    head_out = pl.BlockSpec((None, nh, tm, HEAD_PAD), lambda b, i: (b, 0, i, 0))
    q = pl.pallas_call(
        functools.partial(_q_proj_kernel, n_heads=nh, q_scale=q_scale),
        grid=(batch, nt),
        in_specs=[row, tab, tab, full(w["wdq"]), full(w["qn"]), full(w["wuq_n"]), full(w["wuq_p"])],
        out_specs=head_out,
        out_shape=jax.ShapeDtypeStruct((batch, nh, seq, HEAD_PAD), BF16),
        compiler_params=_cparams(("parallel", "parallel")),
        name="mla_q_proj",
    )(h, rope_c, rope_s, w["wdq"], w["qn"], w["wuq_n"], w["wuq_p"])

    k, vt = pl.pallas_call(
        functools.partial(_kv_proj_kernel, n_heads=nh, kv_lora=KV_LORA),
        grid=(batch, nt),
        in_specs=[row, tab, tab, full(w["wdkv"]), full(w["kvn"]), full(w["wuk"]), full(w["wuvt"])],
        out_specs=[head_out,
                   pl.BlockSpec((None, nh, None, VT_ROWS, tm), lambda b, i: (b, 0, i, 0, 0))],
        out_shape=[jax.ShapeDtypeStruct((batch, nh, seq, HEAD_PAD), BF16),
                   jax.ShapeDtypeStruct((batch, nh, nt, VT_ROWS, tm), BF16)],
        compiler_params=_cparams(("parallel", "parallel")),
        name="mla_kv_proj",
    )(h, rope_c, rope_s, w["wdkv"], w["kvn"], w["wuk"], w["wuvt"])
    return q, k, vt


def _attn_kernel(q_ref, k_ref, vt_ref, o_ref, s_a, s_b, p_a, p_b, *, tq, tk):
    seq = q_ref.shape[0]
    nq = seq // tq
    nk = seq // tk
    nt = (((1,), (1,)), ((), ()))

    def q_body(qi, carry):
        q0 = pl.multiple_of(qi * tq, tq)
        q = q_ref[pl.ds(q0, tq), :]

        def scores_into(j, s_ref):
            k0 = pl.multiple_of(j * tk, tk)
            sc = lax.dot_general(k_ref[pl.ds(k0, tk), :], q, nt, preferred_element_type=F32)
            s_ref[...] = sc
            return jnp.max(sc, axis=0, keepdims=True)

        def stage(j, s_cur, s_next, p_cur, p_prev, st, first=False, last=False):
            m, mx_cur, acc, alpha_prev = st
            mx_next = mx_cur if last else scores_into(j + 1, s_next)
            m_new = jnp.maximum(m, mx_cur)
            alpha = jnp.exp2(m - m_new)
            if not first:
                acc = alpha_prev * acc + jnp.dot(vt_ref[j - 1], p_prev[...], preferred_element_type=F32)
            p_cur[...] = jnp.exp2(s_cur[...] - m_new).astype(BF16)
            return m_new, mx_next, acc, alpha

        def pair(jj, st):
            st = stage(2 * jj + 1, s_b, s_a, p_b, p_a, st)
            return stage(2 * jj + 2, s_a, s_b, p_a, p_b, st)

        mx0 = scores_into(0, s_a)
        st = (jnp.full((1, tq), -1e30, F32), mx0, jnp.zeros((VT_ROWS, tq), F32), jnp.ones((1, tq), F32))
        st = stage(0, s_a, s_b, p_a, p_b, st, first=True)
        st = lax.fori_loop(0, (nk - 2) // 2, pair, st)
        _, _, acc, alpha_last = stage(nk - 1, s_b, s_a, p_b, p_a, st, last=True)
        acc = alpha_last * acc + jnp.dot(vt_ref[nk - 1], p_b[...], preferred_element_type=F32)
        o = acc[0:V_HEAD, :] / acc[V_HEAD:V_HEAD + 1, :]
        o_ref[pl.ds(q0, tq), :] = o.T.astype(BF16)
        return carry

    lax.fori_loop(0, nq, q_body, 0)


def attention(q, k, vt, tq):
    batch, nh, seq, _ = q.shape
    nt, tk = vt.shape[2], vt.shape[4]
    assert nt % 2 == 0
    return pl.pallas_call(
        functools.partial(_attn_kernel, tq=tq, tk=tk),
        grid=(batch, nh),
        in_specs=[pl.BlockSpec((None, None, seq, HEAD_PAD), lambda b, h: (b, h, 0, 0)),
                  pl.BlockSpec((None, None, seq, HEAD_PAD), lambda b, h: (b, h, 0, 0)),
                  pl.BlockSpec((None, None, nt, VT_ROWS, tk), lambda b, h: (b, h, 0, 0, 0))],
        out_specs=pl.BlockSpec((None, seq, V_HEAD), lambda b, h: (b, 0, h)),
        out_shape=jax.ShapeDtypeStruct((batch, seq, nh * V_HEAD), BF16),
        scratch_shapes=[pltpu.VMEM((tk, tq), F32), pltpu.VMEM((tk, tq), F32),
                        pltpu.VMEM((tk, tq), BF16), pltpu.VMEM((tk, tq), BF16)],
        compiler_params=_cparams(("parallel", "parallel")),
        name="mla_attention",
    )(q, k, vt)


def _split_bf16(x):
    hi = x.astype(BF16)
    lo = (x - hi.astype(F32)).astype(BF16)
    return hi, lo


def _route(x, w_t, bias, top_k):
    nt = (((1,), (1,)), ((), ()))
    x_hi, x_lo = _split_bf16(x)
    w_hi, w_lo = _split_bf16(w_t)
    logits = (lax.dot_general(w_hi, x_hi, nt, preferred_element_type=F32)
              + lax.dot_general(w_hi, x_lo, nt, preferred_element_type=F32)
              + lax.dot_general(w_lo, x_hi, nt, preferred_element_type=F32)
              + bias)
    ne = logits.shape[0]
    eid = lax.broadcasted_iota(jnp.int32, logits.shape, 0)
    vals, idxs = [], []
    cur = logits
    for _ in range(top_k):
        mx = jnp.max(cur, axis=0, keepdims=True)
        ix = jnp.min(jnp.where(cur == mx, eid, ne), axis=0, keepdims=True)
        vals.append(mx)
        idxs.append(ix)
        cur = jnp.where(eid == ix, -jnp.inf, cur)
    ex = [jnp.exp(v - vals[0]) for v in vals]
    tot = ex[0]
    for e in ex[1:]:
        tot = tot + e
    return idxs, [e / tot for e in ex]


def _proj_ln_kernel(a_ref, w_ref, bias_ref, h_ref, g_ref, b_ref, wr_ref, br_ref,
                    o_ref, hp_ref, idx_ref, gate_ref):
    m = jnp.dot(a_ref[...], w_ref[...], preferred_element_type=F32) + bias_ref[...]
    o = _layer_norm(DEEPNORM_ALPHA * h_ref[...] + m, g_ref[...], b_ref[...])
    o_ref[...] = o
    _store_token_slabs(hp_ref, 0, _pack_pairs(o))
    idxs, gates = _route(o, wr_ref[...], br_ref[...], TOP_K)
    for kk in range(TOP_K):
        idx_ref[kk:kk + 1, :] = idxs[kk]
        gate_ref[kk:kk + 1, :] = gates[kk]


def proj_residual_ln(a, w, bias, h, g, b, wr_t, br, tm=512):
    n, d = h.shape
    kdim = a.shape[1]
    ne = wr_t.shape[0]
    nb = d // 256
    vec = pl.BlockSpec((1, d), lambda i: (0, 0))
    return pl.pallas_call(
        _proj_ln_kernel,
        grid=(n // tm,),
        in_specs=[pl.BlockSpec((tm, kdim), lambda i: (i, 0)),
                  pl.BlockSpec((kdim, d), lambda i: (0, 0)),
                  vec,
                  pl.BlockSpec((tm, d), lambda i: (i, 0)),
                  vec, vec,
                  pl.BlockSpec((ne, d), lambda i: (0, 0)),
                  pl.BlockSpec((ne, 1), lambda i: (0, 0))],
        out_specs=[pl.BlockSpec((tm, d), lambda i: (i, 0)),
                   pl.BlockSpec((tm * nb, 128), lambda i: (i, 0)),
                   pl.BlockSpec((TOP_K, tm), lambda i: (0, i)),
                   pl.BlockSpec((TOP_K, tm), lambda i: (0, i))],
        out_shape=[jax.ShapeDtypeStruct((n, d), F32),
                   jax.ShapeDtypeStruct((n * nb, 128), jnp.uint32),
                   jax.ShapeDtypeStruct((TOP_K, n), jnp.int32),
                   jax.ShapeDtypeStruct((TOP_K, n), F32)],
        compiler_params=_cparams(("parallel",)),
        name="proj_residual_ln",
    )(a, w, bias.reshape(1, d), h, g.reshape(1, d), b.reshape(1, d), wr_t, br.reshape(ne, 1))


def _chan_dft_kernel(h_ref, t_ref, z_ref, *, groups, gd):
    t = t_ref[...]
    for g in range(groups):
        sl = slice(g * gd, (g + 1) * gd)
        xg = h_ref[:, sl].astype(BF16)
        zz = jnp.dot(xg, t, preferred_element_type=F32)
        z_ref[0, :, sl] = zz[:, :gd].astype(BF16)
        z_ref[1, :, sl] = zz[:, gd:].astype(BF16)


def channel_dft(h, tab, tm):
    n, d = h.shape
    gd = d // FNET_GROUPS
    return pl.pallas_call(
        functools.partial(_chan_dft_kernel, groups=FNET_GROUPS, gd=gd),
        grid=(n // tm,),
        in_specs=[pl.BlockSpec((tm, d), lambda i: (i, 0)),
                  pl.BlockSpec((gd, 2 * gd), lambda i: (0, 0))],
        out_specs=pl.BlockSpec((None, 2, tm, d), lambda i: (i, 0, 0, 0)),
        out_shape=jax.ShapeDtypeStruct((n // tm, 2, tm, d), BF16),
        compiler_params=_cparams(("parallel",)),
        name="fnet_channel_dft",
    )(h, tab)


def _pos_dft_kernel(t_ref, z_ref, o_ref, acc_ref, *, norm):
    k = pl.program_id(2)

    @pl.when(k == 0)
    def _():
        acc_ref[...] = jnp.zeros_like(acc_ref)

    acc_ref[...] += jnp.dot(t_ref[...], z_ref[...], preferred_element_type=F32)

    @pl.when(k == pl.num_programs(2) - 1)
    def _():
        o_ref[...] = (acc_ref[...] * norm).astype(BF16)


def position_dft(tab, z, norm, tm, tk):
    batch, k2, d = z.shape
    seq = tab.shape[0]
    return pl.pallas_call(
        functools.partial(_pos_dft_kernel, norm=norm),
        grid=(batch, seq // tm, k2 // tk),
        in_specs=[pl.BlockSpec((tm, tk), lambda b, i, k: (i, k)),
                  pl.BlockSpec((None, tk, d), lambda b, i, k: (b, k, 0))],
        out_specs=pl.BlockSpec((None, tm, d), lambda b, i, k: (b, i, 0)),
        out_shape=jax.ShapeDtypeStruct((batch, seq, d), BF16),
        scratch_shapes=[pltpu.VMEM((tm, d), F32)],
        compiler_params=_cparams(("parallel", "parallel", "arbitrary")),
        name="fnet_position_dft",
    )(tab, z)


def _swiglu(hg, hu):
    g = jnp.minimum(hg, SWIGLU_LIMIT)
    l = jnp.clip(hu, -SWIGLU_LIMIT, SWIGLU_LIMIT)
    return g * (1.0 / (1.0 + jnp.exp(-SWIGLU_ALPHA * g))) * (l + 1.0)


def _expert_kernel(ue_ref, ustart_ref, ulen_ref, route_ref,
                   hp_hbm, wg_ref, bg_ref, wu_ref, bu_ref, wd_ref, bd_ref,
                   ys_hbm,
                   stage, xb, wgb, wub, wdb, act, ybuf, sem_g, sem_s,
                   *, rows, up_blocks, blocks, nf, tf, n_tok, nb):
    u = pl.program_id(0)
    s = pl.program_id(1)
    ulen = ulen_ref[u]
    in_share = rows // (nf + 1)
    out_share = rows // nf
    half = xb.shape[1] // 2
    n_assign = TOP_K * n_tok

    def row_in(base, i):
        tok = route_ref[base + i] & (ROUTE_TOK - 1)
        return pltpu.make_async_copy(hp_hbm.at[pl.ds(pl.multiple_of(tok * nb, nb), nb), :],
                                     stage.at[pl.ds(pl.multiple_of(i * nb, nb), nb), :], sem_g)

    def row_out(base, i):
        dest = lax.shift_right_logical(route_ref[base + i], ROUTE_SHIFT)
        return pltpu.make_async_copy(ybuf.at[pl.ds(pl.multiple_of(i * nb, nb), nb), :],
                                     ys_hbm.at[pl.ds(pl.multiple_of(dest * nb, nb), nb), :], sem_s)

    def loop_rows(fn):
        def single(i, c):
            fn(i)
            return c
        lax.fori_loop(0, rows, single, 0)

    def wait_rows(make):
        for i in range(rows):
            make(i).wait()

    base_next = ustart_ref[u + 1]
    base_prev = jnp.where(u > 0, ustart_ref[jnp.maximum(u - 1, 0)], n_assign)

    @pl.when(jnp.logical_and(u == 0, s == 0))
    def _first_step():
        ybuf[...] = jnp.zeros_like(ybuf)
        base = ustart_ref[0]
        loop_rows(lambda i: row_in(base, i).start())

    @pl.when(jnp.logical_and(s == 0, ulen > 0))
    def _await_rows():
        base = ustart_ref[u]
        wait_rows(lambda i: row_in(base, i))
        for start, size in up_blocks:
            @pl.when(start < ulen)
            def _():
                r = pl.ds(start, size)
                for t in range(nb):
                    lo, hi = _unpack_pairs(_load_token_slab_block(stage, start, size, nb, t))
                    xb[r, t * 128:(t + 1) * 128] = lo.astype(BF16)
                    xb[r, half + t * 128:half + (t + 1) * 128] = hi.astype(BF16)

    @pl.when(jnp.logical_and(s < nf, ulen > 0))
    def _gate_up():
        wgb[...] = wg_ref[...].astype(BF16)
        wub[...] = wu_ref[...].astype(BF16)
        for start, size in up_blocks:
            @pl.when(start < ulen)
            def _():
                if start == 0:
                    for c in range(in_share):
                        row_in(base_next, s * in_share + c).start()
                    for c in range(out_share):
                        row_out(base_prev, s * out_share + c).start()
                r = pl.ds(start, size)
                x = xb[r, :]
                hg = jnp.dot(x, wgb[...], preferred_element_type=F32) + bg_ref[...]
                hu = jnp.dot(x, wub[...], preferred_element_type=F32) + bu_ref[...]
                act[s, r, :] = _swiglu(hg, hu).astype(BF16)

    @pl.when(jnp.logical_and(s == nf, ulen > 0))
    def _down():
        wdb[...] = wd_ref[...].astype(BF16)
        wait_rows(lambda i: row_out(base_prev, i))
        for c in range(nf * in_share, rows):
            row_in(base_next, c).start()

        def row_block(start, size):
            r = pl.ds(start, size)
            y = bd_ref[...] + jnp.dot(act[0, r, :], wdb[0:tf, :], preferred_element_type=F32)
            for f in range(1, nf):
                y = y + jnp.dot(act[f, r, :], wdb[f * tf:(f + 1) * tf, :], preferred_element_type=F32)
            _store_token_slabs(ybuf, start, _pack_pairs(y))

        row_block(*blocks[0])
        for start, size in blocks[1:]:
            pl.when(start < ulen)(functools.partial(row_block, start, size))

        @pl.when(ulen_ref[u + 1] == 0)
        def _():
            own = ustart_ref[u]
            loop_rows(lambda i: row_out(own, i).start())
            loop_rows(lambda i: row_out(own, i).wait())
            loop_rows(lambda i: row_in(base_next, i).wait())


def moe_experts(hp, layer, ue, ustart, ulen, route, wg, bg, wu, bu, wd, bd, rows, tf):
    nl, ne, d, fdim = wg.shape
    nb = d // 256
    n_tok = hp.shape[0] // nb
    nf = fdim // tf
    n_units = ue.shape[0] - 1
    last_f = nf - 1
    up_blocks = tuple((st, min(sz, rows - st)) for st, sz in MOE_UP_BLOCKS if st < rows)
    blocks = tuple((st, min(sz, rows - st)) for st, sz in MOE_BLOCKS if st < rows)
    assert rows % nf == 0 and rows % (nf + 1) == 0
    assert sum(sz for _, sz in blocks) == rows and sum(sz for _, sz in up_blocks) == rows
    assert n_tok <= ROUTE_TOK

    def f_idx(u, s, ulen_ref):
        return jnp.where(ulen_ref[u] > 0, jnp.minimum(s, last_f), last_f)

    w_in = pl.BlockSpec((None, None, d, tf),
                        lambda u, s, ue, us, ul, od: (layer, ue[u], 0, f_idx(u, s, ul)))
    b_in = pl.BlockSpec((None, None, 1, tf),
                        lambda u, s, ue, us, ul, od: (layer, ue[u], 0, f_idx(u, s, ul)))
    w_dn = pl.BlockSpec((None, None, fdim, d), lambda u, s, ue, us, ul, od: (layer, ue[u], 0, 0))
    b_dn = pl.BlockSpec((None, None, 1, d), lambda u, s, ue, us, ul, od: (layer, ue[u], 0, 0))
    any_spec = pl.BlockSpec(memory_space=pl.ANY)

    grid_spec = pltpu.PrefetchScalarGridSpec(
        num_scalar_prefetch=4,
        grid=(n_units, nf + 1),
        in_specs=[any_spec, w_in, b_in, w_in, b_in, w_dn, b_dn],
        out_specs=any_spec,
        scratch_shapes=[
            pltpu.VMEM((rows * nb, 128), jnp.uint32),
            pltpu.VMEM((rows, d), BF16),
            pltpu.VMEM((d, tf), BF16),
            pltpu.VMEM((d, tf), BF16),
            pltpu.VMEM((fdim, d), BF16),
            pltpu.VMEM((nf, rows, tf), BF16),
            pltpu.VMEM((rows * nb, 128), jnp.uint32),
            pltpu.SemaphoreType.DMA(()),
            pltpu.SemaphoreType.DMA(()),
        ],
    )
    return pl.pallas_call(
        functools.partial(_expert_kernel, rows=rows, up_blocks=up_blocks, blocks=blocks, nf=nf, tf=tf,
                          n_tok=n_tok, nb=nb),
        grid_spec=grid_spec,
        out_shape=jax.ShapeDtypeStruct(((n_tok * TOP_K + rows) * nb, 128), jnp.uint32),
        compiler_params=_cparams(("arbitrary", "arbitrary")),
        name="moe_experts",
    )(ue, ustart, ulen, route, hp, wg, bg.reshape(nl, ne, 1, fdim), wu, bu.reshape(nl, ne, 1, fdim),
      wd, bd.reshape(nl, ne, 1, d))


def routing_tables(idx_t, rows):
    k, n = idx_t.shape
    n_assign = k * n
    flat_e = idx_t.T.reshape(-1)
    order = jnp.argsort(flat_e, stable=True).astype(jnp.int32)
    dest = (order % k) * n + order // k
    spare = n_assign + jnp.arange(rows, dtype=jnp.int32)
    route = jnp.concatenate([dest * ROUTE_TOK + order // k, spare * ROUTE_TOK])
    eids = jnp.arange(N_EXPERTS, dtype=jnp.int32)
    counts = jnp.sum(flat_e[:, None] == eids[None, :], axis=0, dtype=jnp.int32)
    cstart = jnp.cumsum(counts) - counts
    nu = (counts + rows - 1) // rows
    ucum = jnp.cumsum(nu)
    per = (counts + jnp.maximum(nu, 1) - 1) // jnp.maximum(nu, 1)
    per = ((per + 7) // 8) * 8
    n_units = n_assign // rows + N_EXPERTS
    uid = jnp.arange(n_units + 1, dtype=jnp.int32)
    valid = uid < ucum[-1]
    e_last = jnp.sum(ucum < ucum[-1], dtype=jnp.int32)
    e_u = jnp.where(valid, jnp.sum(uid[:, None] >= ucum[None, :], axis=1, dtype=jnp.int32), e_last)
    onehot = (e_u[:, None] == eids[None, :]).astype(jnp.int32)

    def of_unit(v):
        return jnp.sum(onehot * v[None, :], axis=1)

    j = uid - of_unit(ucum - nu)
    per_u = of_unit(per)
    ustart = jnp.where(valid, of_unit(cstart) + j * per_u, 0)
    ulen = jnp.where(valid, jnp.clip(of_unit(counts) - j * per_u, 0, per_u), 0)
    return e_u, ustart, ulen, route


def _combine_ln_kernel(y0, y1, y2, y3, gt_ref, h_ref, g_ref, b_ref, o_ref, f_ref, *, nb):
    tm, d = h_ref.shape
    half = d // 2
    gt = gt_ref[...]
    slots = (y0, y1, y2, y3)
    for t in range(nb):
        lo_sum = hi_sum = None
        for kk, y_ref in enumerate(slots):
            lo, hi = _unpack_pairs(_load_token_slab_block(y_ref, 0, tm, nb, t))
            gk = gt[:, kk:kk + 1]
            lo_sum = lo * gk if lo_sum is None else lo_sum + lo * gk
            hi_sum = hi * gk if hi_sum is None else hi_sum + hi * gk
        f_ref[:, t * 128:(t + 1) * 128] = lo_sum
        f_ref[:, half + t * 128:half + (t + 1) * 128] = hi_sum
    o_ref[...] = _layer_norm(DEEPNORM_ALPHA * h_ref[...] + f_ref[...], g_ref[...], b_ref[...])


def combine_residual_ln(ys, gates, h, g, b, tm=256):
    n, d = h.shape
    nb = d // 256
    nblk = n // tm
    vec = pl.BlockSpec((1, d), lambda i: (0, 0))

    def slot(kk):
        return pl.BlockSpec((tm * nb, 128), lambda i: (kk * nblk + i, 0))

    return pl.pallas_call(
        functools.partial(_combine_ln_kernel, nb=nb),
        grid=(nblk,),
        in_specs=[slot(0), slot(1), slot(2), slot(3),
                  pl.BlockSpec((tm, TOP_K), lambda i: (i, 0)),
                  pl.BlockSpec((tm, d), lambda i: (i, 0)), vec, vec],
        out_specs=pl.BlockSpec((tm, d), lambda i: (i, 0)),
        out_shape=jax.ShapeDtypeStruct((n, d), F32),
        scratch_shapes=[pltpu.VMEM((tm, d), F32)],
        compiler_params=_cparams(("parallel",)),
        name="moe_combine_ln",
    )(ys, ys, ys, ys, gates, h, g.reshape(1, d), b.reshape(1, d))


def _prep_mla(w_dq, q_norm, w_uq, w_dkv, kv_norm, w_ukv, w_o):
    nh = N_HEADS
    d = w_dq.shape[0]
    wuq = w_uq.reshape(Q_LORA, nh, QK_NOPE + QK_ROPE)
    pe = wuq[:, :, QK_NOPE:]
    z = jnp.zeros((Q_LORA, nh, QK_ROPE // 2), F32)
    wuq_p = jnp.concatenate([pe[:, :, 0::2], z, pe[:, :, 1::2], z], axis=-1)
    kpe = w_dkv[:, KV_LORA:]
    zk = jnp.zeros((d, QK_ROPE // 2), F32)
    wdkv = jnp.concatenate([w_dkv[:, :KV_LORA], kpe[:, 0::2], zk, kpe[:, 1::2], zk], axis=1)
    wukv = w_ukv.reshape(KV_LORA, nh, QK_NOPE + V_HEAD)
    return {
        "wdq": w_dq.astype(BF16),
        "qn": q_norm.reshape(1, Q_LORA),
        "wuq_n": wuq[:, :, :QK_NOPE].reshape(Q_LORA, nh * 128).astype(BF16),
        "wuq_p": wuq_p.reshape(Q_LORA, nh * 128).astype(BF16),
        "wdkv": wdkv.astype(BF16),
        "kvn": kv_norm.reshape(1, KV_LORA),
        "wuk": wukv[:, :, :QK_NOPE].reshape(KV_LORA, nh * 128).astype(BF16),
        "wuvt": wukv[:, :, QK_NOPE:].reshape(KV_LORA, nh * V_HEAD).T.astype(BF16),
        "wo": w_o.astype(BF16),
    }


def _rope_tables(positions):
    inv_freq = 1.0 / (ROPE_THETA ** (jnp.arange(0, QK_ROPE, 2, dtype=F32) / QK_ROPE))
    ang = positions.astype(F32).reshape(-1)[:, None] * inv_freq
    c, s = jnp.cos(ang), jnp.sin(ang)
    z = jnp.zeros_like(c)
    return jnp.concatenate([c, z, c, z], axis=1), jnp.concatenate([-s, z, s, z], axis=1)


def _dft_tables(seq, gd, tk):
    def cs(n):
        i = jnp.arange(n, dtype=jnp.int32)
        ang = ((i[:, None] * i[None, :]) % n).astype(F32) * (2.0 * math.pi / n)
        return jnp.cos(ang), jnp.sin(ang)

    cc, sc = cs(gd)
    chan = jnp.concatenate([cc, sc], axis=1).astype(BF16)

    a = 1 << ((seq.bit_length() - 1) // 2)
    col = jnp.arange(2 * seq, dtype=jnp.int32)
    within = col % (2 * tk)
    key = (col // (2 * tk)) * tk + within % tk
    quarter = (within >= tk).astype(jnp.int32) * (seq // 4)
    j1 = jnp.arange(seq // a, dtype=jnp.int32)[:, None]
    j0 = jnp.arange(a, dtype=jnp.int32)[:, None]
    ang_a = ((a * j1 * key[None, :]) % seq).astype(F32) * (2.0 * math.pi / seq)
    ang_b = ((j0 * key[None, :] + quarter[None, :]) % seq).astype(F32) * (2.0 * math.pi / seq)
    pos = (jnp.cos(ang_a)[:, None, :] * jnp.cos(ang_b)[None, :, :]
           - jnp.sin(ang_a)[:, None, :] * jnp.sin(ang_b)[None, :, :])
    return chan, pos.reshape(seq, 2 * seq).astype(BF16)


def kernel(x, positions, ln_in_g, ln_in_b, mla_w_dq, mla_q_norm, mla_w_uq, mla_w_dkv, mla_kv_norm,
           mla_w_ukv, mla_w_o, fnet_w, fnet_b, ln_mix_g, ln_mix_b, router_w, router_b, exp_w_gate,
           exp_b_gate, exp_w_up, exp_b_up, exp_w_down, exp_b_down, ln_moe_g, ln_moe_b):
    batch, seq, d = x.shape
    n = batch * seq
    t_seq = 512
    moe_tf = 256

    rope_c, rope_s = _rope_tables(positions)
    gd = d // FNET_GROUPS
    chan_tab, pos_tab = _dft_tables(seq, gd, t_seq)
    dft_norm = 1.0 / math.sqrt(seq * gd)
    zero_bias = jnp.zeros((d,), F32)

    h = layer_norm_rows(x.reshape(n, d), ln_in_g, ln_in_b)
    for i in range(DEPTH):
        j = i // 2
        if i % 2 == 0:
            w = _prep_mla(mla_w_dq[j], mla_q_norm[j], mla_w_uq[j], mla_w_dkv[j], mla_kv_norm[j],
                          mla_w_ukv[j], mla_w_o[j])
            q, k, vt = mla_projections(h, rope_c, rope_s, w, batch, seq, t_seq)
            o = attention(q, k, vt, tq=min(seq, 4096))
            h, hp, idx_t, gate_t = proj_residual_ln(
                o.reshape(n, N_HEADS * V_HEAD), w["wo"], zero_bias, h, ln_mix_g[i], ln_mix_b[i],
                router_w[i].T, router_b[i])
        else:
            z = channel_dft(h, chan_tab, t_seq).reshape(batch, 2 * seq, d)
            mixed = position_dft(pos_tab, z, dft_norm, tm=1024, tk=2 * t_seq)
            h, hp, idx_t, gate_t = proj_residual_ln(
                mixed.reshape(n, d), fnet_w[j].astype(BF16), fnet_b[j], h, ln_mix_g[i], ln_mix_b[i],
                router_w[i].T, router_b[i])
        ue, ustart, ulen, route = routing_tables(idx_t, MOE_ROWS)
        ys = moe_experts(hp, i, ue, ustart, ulen, route, exp_w_gate, exp_b_gate, exp_w_up,
                         exp_b_up, exp_w_down, exp_b_down, MOE_ROWS, moe_tf)
        h = combine_residual_ln(ys, gate_t.T, h, ln_moe_g[i], ln_moe_b[i])
    return h.reshape(batch, seq, d)
```
